```python
import jax, jax.numpy as jnp
from jax import lax
import numpy as np

D_MODEL = 4096
BATCH = 4
SEQ = 4096
DEPTH = 2

GRID_W = 64
CTX_LEN = 256
EPS = 1e-6
NEG_INF = -1e30
N_MOD = 6

MIX_W = D_MODEL
GLA_HEADS = 4
GLA_V_W = MIX_W // 2
GLA_DV = GLA_V_W // GLA_HEADS
GLA_K_W = GLA_V_W // 2
GLA_DK = GLA_K_W // GLA_HEADS
GLA_RANK = 16
GLA_TAU = 16.0
GLA_CHUNK = 64
SG_GROUPS = 4
SG_W = MIX_W // 2
SG_GC = SG_W // SG_GROUPS
SG_CHUNK = 128
AB_WIDTHS = (GLA_K_W, GLA_K_W, GLA_V_W, GLA_V_W, GLA_RANK, GLA_RANK, SG_W, SG_W)
AB_IN_W = sum(AB_WIDTHS)
ATT_HEADS = D_MODEL // 128
KV_HEADS = 8
HEAD_DIM = 128
GQ = ATT_HEADS // KV_HEADS
WINDOW = 128
ATT_BLOCK = 128
BAND = ATT_BLOCK + 2 * WINDOW
AXIS_DIM = HEAD_DIM // 2
ROPE_BASE = 10000.0
Q_W = ATT_HEADS * HEAD_DIM
KV_W = KV_HEADS * HEAD_DIM
C_IN_W = Q_W + 2 * KV_W
N_EXPERTS = 16
EXPERT_FF = D_MODEL // 4
EC_CAPACITY = 2

kernel_name = 'hybrid_gla_sgmlp_swa_ecmoe_dit'


def rms_norm(x, gain):
    xf = x.astype(jnp.float32)
    y = xf * lax.rsqrt(jnp.mean(xf * xf, axis=-1, keepdims=True) + EPS)
    return (y * gain.astype(jnp.float32)).astype(x.dtype)


def modulate(x, gain, shift, scale):
    return rms_norm(x, gain) * (1 + scale) + shift


def split_cols(p, widths):
    return jnp.split(p, [int(i) for i in np.cumsum(widths)[:-1]], axis=-1)


def gla_chunked(q, k, v, log_a, s0):
    B, T, H, dk = q.shape
    dv = v.shape[-1]
    n = T // GLA_CHUNK
    shp = lambda a: a.astype(jnp.float32).reshape(B, n, GLA_CHUNK, H, a.shape[-1])
    q, k, v, log_a = shp(q), shp(k), shp(v), shp(log_a)
    b = jnp.cumsum(log_a, axis=2)
    b_last = b[:, :, -1:]
    q_in = q * jnp.exp(b)
    k_in = k * jnp.exp(-b)
    k_dec = k * jnp.exp(b_last - b)
    causal = jnp.tril(jnp.ones((GLA_CHUNK, GLA_CHUNK), bool))
    scores = jnp.where(causal, jnp.einsum('bnihd,bnjhd->bnhij', q_in, k_in), 0.0)
    o_intra = jnp.einsum('bnhij,bnjhe->bnihe', scores, v)
    decay = jnp.exp(b_last[:, :, 0])

    def step(S, xs):
        qc, kc, vc, dc = xs
        o = jnp.einsum('bihd,bhde->bihe', qc, S)
        S = dc[..., None] * S + jnp.einsum('bjhd,bjhe->bhde', kc, vc)
        return S, o

    xs = tuple(jnp.moveaxis(a, 1, 0) for a in (q_in, k_dec, v, decay))
    S, o_inter = lax.scan(step, s0.astype(jnp.float32), xs)
    o = o_intra + jnp.moveaxis(o_inter, 0, 1)
    return o.reshape(B, T, H, dv), S


def gla_bidir(q, k, v, la_f, la_b, s_f, s_b):
    rev = lambda a: jnp.flip(a, axis=1)
    o_f, s_f = gla_chunked(q, k, v, la_f, s_f)
    o_b, s_b = gla_chunked(rev(q), rev(k), rev(v), rev(la_b), s_b)
    return o_f + rev(o_b), s_f, s_b


def gla_output(o, g, gain):
    B, T = o.shape[:2]
    y = o * lax.rsqrt(jnp.mean(o * o, axis=-1, keepdims=True) + EPS)
    y = y * gain.astype(jnp.float32).reshape(GLA_HEADS, GLA_DV)
    return y.reshape(B, T, GLA_V_W).astype(g.dtype) * jax.nn.silu(g)


def spatial_gate(u, vs, nw, nb, ws, bs):
    B, T, _ = u.shape
    u = jax.nn.gelu(u)
    vf = jax.nn.gelu(vs).astype(jnp.float32).reshape(B, T, SG_GROUPS, SG_GC)
    mu = jnp.mean(vf, axis=-1, keepdims=True)
    var = jnp.mean(jnp.square(vf - mu), axis=-1, keepdims=True)
    vn = (vf - mu) * lax.rsqrt(var + EPS) * nw.astype(jnp.float32).reshape(SG_GROUPS, SG_GC) + nb.astype(jnp.float32).reshape(SG_GROUPS, SG_GC)
    vn = vn.astype(u.dtype).reshape(B, T // SG_CHUNK, SG_CHUNK, SG_GROUPS, SG_GC)
    s = jnp.einsum('gij,bnjgc->bnigc', ws, vn) + bs.T[:, :, None]
    return u * s.reshape(B, T, SG_W)


def ab_project(x, w_in, gate_w, gate_b):
    B, T, _ = x.shape
    q, k, v, g, lr_f, lr_b, u, vs = split_cols(x @ w_in, AB_WIDTHS)
    heads = lambda a, d: a.reshape(B, T, GLA_HEADS, d)
    log_decay = lambda lr, d: jax.nn.log_sigmoid((lr @ gate_w[d] + gate_b[d]).astype(jnp.float32)) / GLA_TAU
    gla_in = (heads(q, GLA_DK) * GLA_DK ** -0.5, heads(k, GLA_DK), heads(v, GLA_DV),
              heads(log_decay(lr_f, 0), GLA_DK), heads(log_decay(lr_b, 1), GLA_DK))
    return gla_in, g, u, vs


def ab_mixer(xl, xc, w_in, gate_w, gate_b, gla_gain, sg_nw, sg_nb, sg_ws, sg_bs, w_out):
    gla_c, g_c, u_c, vs_c = ab_project(xc, w_in, gate_w, gate_b)
    gla_l, g_l, u_l, vs_l = ab_project(xl, w_in, gate_w, gate_b)
    s0 = jnp.zeros((xc.shape[0], GLA_HEADS, GLA_DK, GLA_DV), jnp.float32)
    o_c, s_f, s_b = gla_bidir(*gla_c, s0, s0)
    o_l, _, _ = gla_bidir(*gla_l, s_f, s_b)

    def merge(o, g, u, vs):
        a = gla_output(o, g, gla_gain)
        b = spatial_gate(u, vs, sg_nw, sg_nb, sg_ws, sg_bs)
        return jnp.concatenate([a, b], axis=-1) @ w_out

    return merge(o_l, g_l, u_l, vs_l), merge(o_c, g_c, u_c, vs_c)


def axial_rope_tables(T, dtype):
    rows = T // GRID_W
    row, col = jnp.meshgrid(jnp.arange(rows), jnp.arange(GRID_W), indexing='ij')
    inv_freq = ROPE_BASE ** (-jnp.arange(0, AXIS_DIM, 2, dtype=jnp.float32) / AXIS_DIM)

    def angles(pos):
        ang = pos.reshape(-1).astype(jnp.float32)[:, None] * inv_freq
        return jnp.concatenate([ang, ang], axis=-1)

    ang = jnp.concatenate([angles(row), angles(col)], axis=-1)
    return jnp.cos(ang).astype(dtype), jnp.sin(ang).astype(dtype)


def apply_axial_rope(x, cos, sin):
    def rot(p):
        p1, p2 = jnp.split(p, 2, axis=-1)
        return jnp.concatenate([-p2, p1], axis=-1)
    x_row, x_col = jnp.split(x, 2, axis=-1)
    return x * cos[:, None] + jnp.concatenate([rot(x_row), rot(x_col)], axis=-1) * sin[:, None]


def window_sink_attention(q, k, v, kc, vc, sinks):
    B, T = q.shape[:2]
    n = T // ATT_BLOCK
    scale = HEAD_DIM ** -0.5
    qb = q.reshape(B, n, ATT_BLOCK, KV_HEADS, GQ, HEAD_DIM).transpose(1, 0, 2, 3, 4, 5)
    pad = ((0, 0), (WINDOW, WINDOW), (0, 0), (0, 0))
    k_pad, v_pad = jnp.pad(k, pad), jnp.pad(v, pad)
    sink_logit = sinks.astype(jnp.float32).reshape(1, KV_HEADS, GQ, 1, 1)
    rel = jnp.arange(BAND)[None, :] - jnp.arange(ATT_BLOCK)[:, None]
    in_window = (rel >= 0) & (rel <= 2 * WINDOW)

    def block(args):
        qi, idx = args
        start = idx * ATT_BLOCK
        kb = lax.dynamic_slice_in_dim(k_pad, start, BAND, axis=1)
        vb = lax.dynamic_slice_in_dim(v_pad, start, BAND, axis=1)
        key_pos = start - WINDOW + jnp.arange(BAND)
        valid = in_window & ((key_pos >= 0) & (key_pos < T))[None, :]
        s_loc = jnp.einsum('bihgd,bjhd->bhgij', qi, kb).astype(jnp.float32) * scale
        s_loc = jnp.where(valid, s_loc, NEG_INF)
        s_ctx = jnp.einsum('bihgd,bjhd->bhgij', qi, kc).astype(jnp.float32) * scale
        sink = jnp.broadcast_to(sink_logit, s_loc.shape[:-1] + (1,))
        p = jax.nn.softmax(jnp.concatenate([s_loc, s_ctx, sink], axis=-1), axis=-1).astype(v.dtype)
        L = kc.shape[1]
        return (jnp.einsum('bhgij,bjhd->bihgd', p[..., :BAND], vb)
                + jnp.einsum('bhgij,bjhd->bihgd', p[..., BAND:BAND + L], vc))

    o = lax.map(block, (qb, jnp.arange(n)))
    return o.transpose(1, 0, 2, 3, 4, 5).reshape(B, T, Q_W)


def ctx_sink_attention(qc, kc, vc, sinks):
    B, L = qc.shape[:2]
    qg = qc.reshape(B, L, KV_HEADS, GQ, HEAD_DIM)
    s = jnp.einsum('bihgd,bjhd->bhgij', qg, kc).astype(jnp.float32) * HEAD_DIM ** -0.5
    sink = jnp.broadcast_to(sinks.astype(jnp.float32).reshape(1, KV_HEADS, GQ, 1, 1), s.shape[:-1] + (1,))
    p = jax.nn.softmax(jnp.concatenate([s, sink], axis=-1), axis=-1)[..., :-1].astype(vc.dtype)
    return jnp.einsum('bhgij,bjhd->bihgd', p, vc).reshape(B, L, Q_W)


def c_mixer(xl, xc, w_in, sinks, w_out, cos, sin, need_ctx):
    B, T, _ = xl.shape
    L = xc.shape[1]
    q, k, v = split_cols(xl @ w_in, (Q_W, KV_W, KV_W))
    q = apply_axial_rope(q.reshape(B, T, ATT_HEADS, HEAD_DIM), cos, sin)
    k = apply_axial_rope(k.reshape(B, T, KV_HEADS, HEAD_DIM), cos, sin)
    v = v.reshape(B, T, KV_HEADS, HEAD_DIM)
    kc, vc = split_cols(xc @ w_in[:, Q_W:], (KV_W, KV_W))
    kc = kc.reshape(B, L, KV_HEADS, HEAD_DIM)
    vc = vc.reshape(B, L, KV_HEADS, HEAD_DIM)
    y_l = window_sink_attention(q, k, v, kc, vc, sinks) @ w_out
    y_c = None
    if need_ctx:
        qc = (xc @ w_in[:, :Q_W]).reshape(B, L, ATT_HEADS, HEAD_DIM)
        y_c = ctx_sink_attention(qc, kc, vc, sinks) @ w_out
    return y_l, y_c


def ec_moe(h, router_w, w_gate, w_up, w_down):
    B, T, D = h.shape
    cap = EC_CAPACITY * T // N_EXPERTS
    aff = jax.nn.softmax((h @ router_w).astype(jnp.float32), axis=-1)
    gates, idx = lax.top_k(jnp.swapaxes(aff, 1, 2), cap)
    xg = jax.vmap(lambda hb, ib: hb[ib])(h, idx)
    hid = jax.nn.silu(jnp.einsum('becd,edf->becf', xg, w_gate)) * jnp.einsum('becd,edf->becf', xg, w_up)
    y = jnp.einsum('becf,efd->becd', hid, w_down) * gates[..., None].astype(h.dtype)
    return jax.vmap(lambda yb, ib: jnp.zeros((T, D), h.dtype).at[ib.reshape(-1)].add(yb.reshape(-1, D)))(y, idx)


def setup_inputs(seed: int = 0) -> dict:
    key = jax.random.key(seed)
    ks = iter(jax.random.split(key, 25))
    n_even = (DEPTH + 1) // 2
    n_odd = DEPTH // 2

    def normal(shape, s=1.0):
        return jax.random.normal(next(ks), shape, jnp.float32) * s

    def dense(shape, fan_in, gain=1.0):
        return normal(shape, gain * fan_in ** -0.5)

    def gain(shape):
        return 1.0 + normal(shape, 0.1)

    return {
        'x': normal((BATCH, SEQ, D_MODEL)),
        'c': normal((BATCH, D_MODEL)),
        'ctx': normal((BATCH, CTX_LEN, D_MODEL)),
        'c_ctx': normal((D_MODEL,)),
        'mod_w': dense((DEPTH, D_MODEL, N_MOD * D_MODEL), D_MODEL, 0.5),
        'mod_b': normal((DEPTH, N_MOD * D_MODEL), 0.02),
        'norm_mix': gain((DEPTH, D_MODEL)),
        'norm_ffn': gain((DEPTH, D_MODEL)),
        'ab_w_in': dense((n_even, D_MODEL, AB_IN_W), D_MODEL),
        'gla_gate_w': dense((n_even, 2, GLA_RANK, GLA_K_W), GLA_RANK),
        'gla_gate_b': normal((n_even, 2, GLA_K_W), 0.1),
        'gla_norm': gain((n_even, GLA_V_W)),
        'sg_norm_w': gain((n_even, SG_W)),
        'sg_norm_b': normal((n_even, SG_W), 0.02),
        'sg_ws': dense((n_even, SG_GROUPS, SG_CHUNK, SG_CHUNK), SG_CHUNK),
        'sg_bs': gain((n_even, SG_GROUPS, SG_CHUNK)),
        'ab_w_out': dense((n_even, MIX_W, D_MODEL), MIX_W),
        'c_w_in': dense((n_odd, D_MODEL, C_IN_W), D_MODEL),
        'sinks': normal((n_odd, ATT_HEADS), 0.5),
        'c_w_out': dense((n_odd, Q_W, D_MODEL), Q_W),
        'router_w': dense((DEPTH, D_MODEL, N_EXPERTS), D_MODEL),
        'moe_w_gate': dense((DEPTH, N_EXPERTS, D_MODEL, EXPERT_FF), D_MODEL),
        'moe_w_up': dense((DEPTH, N_EXPERTS, D_MODEL, EXPERT_FF), D_MODEL),
        'moe_w_down': dense((DEPTH, N_EXPERTS, EXPERT_FF, D_MODEL), EXPERT_FF),
        'final_norm': gain((D_MODEL,)),
    }


def reference(x, c, ctx, c_ctx, mod_w, mod_b, norm_mix, norm_ffn, ab_w_in, gla_gate_w, gla_gate_b, gla_norm,
              sg_norm_w, sg_norm_b, sg_ws, sg_bs, ab_w_out, c_w_in, sinks, c_w_out, router_w, moe_w_gate,
              moe_w_up, moe_w_down, final_norm):
    B, T, D = x.shape
    cos, sin = axial_rope_tables(T, x.dtype)
    h, hc = x, ctx
    for layer in range(DEPTH):
        last = layer == DEPTH - 1
        m_l = (jax.nn.silu(c) @ mod_w[layer] + mod_b[layer]).reshape(B, N_MOD, 1, D)
        m_l = [m_l[:, i] for i in range(N_MOD)]
        m_c = (jax.nn.silu(c_ctx) @ mod_w[layer] + mod_b[layer]).reshape(N_MOD, 1, 1, D)
        xl = modulate(h, norm_mix[layer], m_l[0], m_l[1])
        xc = modulate(hc, norm_mix[layer], m_c[0], m_c[1])
        if layer % 2 == 0:
            e = layer // 2
            y_l, y_c = ab_mixer(xl, xc, ab_w_in[e], gla_gate_w[e], gla_gate_b[e], gla_norm[e], sg_norm_w[e],
                                sg_norm_b[e], sg_ws[e], sg_bs[e], ab_w_out[e])
        else:
            o = layer // 2
            y_l, y_c = c_mixer(xl, xc, c_w_in[o], sinks[o], c_w_out[o], cos, sin, not last)
        moe = (router_w[layer], moe_w_gate[layer], moe_w_up[layer], moe_w_down[layer])
        h = h + m_l[2] * y_l
        h = h + m_l[5] * ec_moe(modulate(h, norm_ffn[layer], m_l[3], m_l[4]), *moe)
        if not last:
            hc = hc + m_c[2] * y_c
            hc = hc + m_c[5] * ec_moe(modulate(hc, norm_ffn[layer], m_c[3], m_c[4]), *moe)
    return rms_norm(h, final_norm)
```

```python
import functools

import jax
import jax.numpy as jnp
import numpy as np
from jax import lax
from jax.experimental import pallas as pl
from jax.experimental.pallas import tpu as pltpu

D_MODEL = 4096
DEPTH = 2
GRID_W = 64
EPS = 1e-6
NEG_INF = -1e30
N_MOD = 6
MIX_W = D_MODEL
GLA_HEADS = 4
GLA_V_W = MIX_W // 2
GLA_DV = GLA_V_W // GLA_HEADS
GLA_K_W = GLA_V_W // 2
GLA_DK = GLA_K_W // GLA_HEADS
GLA_RANK = 16
GLA_TAU = 16.0
GLA_CHUNK = 64
SG_GROUPS = 4
SG_W = MIX_W // 2
SG_GC = SG_W // SG_GROUPS
SG_CHUNK = 128
AB_WIDTHS = (GLA_K_W, GLA_K_W, GLA_V_W, GLA_V_W, GLA_RANK, GLA_RANK, SG_W, SG_W)
ATT_HEADS = D_MODEL // 128
KV_HEADS = 8
HEAD_DIM = 128
GQ = ATT_HEADS // KV_HEADS
WINDOW = 128
ATT_BLOCK = 128
BAND = ATT_BLOCK + 2 * WINDOW
AXIS_DIM = HEAD_DIM // 2
ROPE_BASE = 10000.0
Q_W = ATT_HEADS * HEAD_DIM
KV_W = KV_HEADS * HEAD_DIM
N_EXPERTS = 16
EC_CAPACITY = 2

V7X_VMEM_LIMIT_BYTES = 56 * 1024 * 1024


def _cparams(*sem):
    return pltpu.CompilerParams(dimension_semantics=sem, vmem_limit_bytes=V7X_VMEM_LIMIT_BYTES)


def _mm_body(x_ref, w_ref, o_ref):
    o_ref[...] = jnp.dot(x_ref[...], w_ref[...], preferred_element_type=jnp.float32).astype(o_ref.dtype)


def _mm(x, w, tm, tn, out_dtype=jnp.float32):
    m, k = x.shape
    n = w.shape[1]
    assert m % tm == 0 and n % tn == 0, (m, n, tm, tn)
    return pl.pallas_call(
        _mm_body,
        grid=(m // tm, n // tn),
        in_specs=[pl.BlockSpec((tm, k), lambda i, j: (i, 0)),
                  pl.BlockSpec((k, tn), lambda i, j: (0, j))],
        out_specs=pl.BlockSpec((tm, tn), lambda i, j: (i, j)),
        out_shape=jax.ShapeDtypeStruct((m, n), out_dtype),
        compiler_params=_cparams("parallel", "arbitrary"),
        name="mm",
    )(x, w)


def _norm_mod_body(x_ref, g_ref, sh_ref, sc_ref, o_ref):
    x = x_ref[...]
    y = x * lax.rsqrt(jnp.mean(x * x, axis=-1, keepdims=True) + EPS)
    y = (y * g_ref[...]) * (1.0 + sc_ref[0]) + sh_ref[0]
    o_ref[...] = y.astype(o_ref.dtype)


def _norm_mod(h, gain, shift, scale, rows_per_group, tm, out_dtype):
    m, d = h.shape
    g = shift.shape[0]
    per = rows_per_group // tm
    grp = lambda i: (jnp.minimum(i // per, g - 1), 0, 0)
    return pl.pallas_call(
        _norm_mod_body,
        grid=(m // tm,),
        in_specs=[pl.BlockSpec((tm, d), lambda i: (i, 0)),
                  pl.BlockSpec((1, d), lambda i: (0, 0)),
                  pl.BlockSpec((1, 1, d), grp),
                  pl.BlockSpec((1, 1, d), grp)],
        out_specs=pl.BlockSpec((tm, d), lambda i: (i, 0)),
        out_shape=jax.ShapeDtypeStruct((m, d), out_dtype),
        compiler_params=_cparams("parallel"),
        name="norm_mod",
    )(h, gain.reshape(1, d), shift, scale)


def _split_cols(p, widths):
    return jnp.split(p, [int(i) for i in np.cumsum(widths)[:-1]], axis=-1)


def _gla_chunked(q, k, v, log_a, s0):
    B, T, H, dk = q.shape
    dv = v.shape[-1]
    n = T // GLA_CHUNK
    shp = lambda a: a.astype(jnp.float32).reshape(B, n, GLA_CHUNK, H, a.shape[-1])
    q, k, v, log_a = shp(q), shp(k), shp(v), shp(log_a)
    b = jnp.cumsum(log_a, axis=2)
    b_last = b[:, :, -1:]
    q_in = q * jnp.exp(b)
    k_in = k * jnp.exp(-b)
    k_dec = k * jnp.exp(b_last - b)
    causal = jnp.tril(jnp.ones((GLA_CHUNK, GLA_CHUNK), bool))
    scores = jnp.where(causal, jnp.einsum('bnihd,bnjhd->bnhij', q_in, k_in), 0.0)
    o_intra = jnp.einsum('bnhij,bnjhe->bnihe', scores, v)
    decay = jnp.exp(b_last[:, :, 0])

    def step(S, xs):
        qc, kc, vc, dc = xs
        o = jnp.einsum('bihd,bhde->bihe', qc, S)
        S = dc[..., None] * S + jnp.einsum('bjhd,bjhe->bhde', kc, vc)
        return S, o

    xs = tuple(jnp.moveaxis(a, 1, 0) for a in (q_in, k_dec, v, decay))
    S, o_inter = lax.scan(step, s0.astype(jnp.float32), xs)
    o = o_intra + jnp.moveaxis(o_inter, 0, 1)
    return o.reshape(B, T, H, dv), S


def _gla_bidir(q, k, v, la_f, la_b, s_f, s_b):
    rev = lambda a: jnp.flip(a, axis=1)
    o_f, s_f = _gla_chunked(q, k, v, la_f, s_f)
    o_b, s_b = _gla_chunked(rev(q), rev(k), rev(v), rev(la_b), s_b)
    return o_f + rev(o_b), s_f, s_b


def _gla_output(o, g, gain):
    B, T = o.shape[:2]
    y = o * lax.rsqrt(jnp.mean(o * o, axis=-1, keepdims=True) + EPS)
    y = y * gain.astype(jnp.float32).reshape(GLA_HEADS, GLA_DV)
    return y.reshape(B, T, GLA_V_W).astype(g.dtype) * jax.nn.silu(g)


def _spatial_gate(u, vs, nw, nb, ws, bs):
    B, T, _ = u.shape
    u = jax.nn.gelu(u)
    vf = jax.nn.gelu(vs).astype(jnp.float32).reshape(B, T, SG_GROUPS, SG_GC)
    mu = jnp.mean(vf, axis=-1, keepdims=True)
    var = jnp.mean(jnp.square(vf - mu), axis=-1, keepdims=True)
    vn = (vf - mu) * lax.rsqrt(var + EPS) * nw.reshape(SG_GROUPS, SG_GC) + nb.reshape(SG_GROUPS, SG_GC)
    vn = vn.reshape(B, T // SG_CHUNK, SG_CHUNK, SG_GROUPS, SG_GC)
    s = jnp.einsum('gij,bnjgc->bnigc', ws, vn) + bs.T[:, :, None]
    return u * s.reshape(B, T, SG_W)


def _ab_split(p, lr, gate_w, gate_b):
    B, T, _ = p.shape
    q, k, v, g, u, vs = _split_cols(p, (GLA_K_W, GLA_K_W, GLA_V_W, GLA_V_W, SG_W, SG_W))
    lr_f, lr_b = lr[..., :GLA_RANK], lr[..., GLA_RANK:2 * GLA_RANK]
    heads = lambda a, d: a.reshape(B, T, GLA_HEADS, d)
    log_decay = lambda l, d: jax.nn.log_sigmoid((l @ gate_w[d] + gate_b[d]).astype(jnp.float32)) / GLA_TAU
    gla_in = (heads(q, GLA_DK) * GLA_DK ** -0.5, heads(k, GLA_DK), heads(v, GLA_DV),
              heads(log_decay(lr_f, 0), GLA_DK), heads(log_decay(lr_b, 1), GLA_DK))
    return gla_in, g, u, vs


def _axial_rope_tables(T, dtype):
    rows = T // GRID_W
    row, col = jnp.meshgrid(jnp.arange(rows), jnp.arange(GRID_W), indexing='ij')
    inv_freq = ROPE_BASE ** (-jnp.arange(0, AXIS_DIM, 2, dtype=jnp.float32) / AXIS_DIM)

    def angles(pos):
        ang = pos.reshape(-1).astype(jnp.float32)[:, None] * inv_freq
        return jnp.concatenate([ang, ang], axis=-1)

    ang = jnp.concatenate([angles(row), angles(col)], axis=-1)
    return jnp.cos(ang).astype(dtype), jnp.sin(ang).astype(dtype)


def _apply_axial_rope(x, cos, sin):
    def rot(p):
        p1, p2 = jnp.split(p, 2, axis=-1)
        return jnp.concatenate([-p2, p1], axis=-1)
    x_row, x_col = jnp.split(x, 2, axis=-1)
    return x * cos[:, None] + jnp.concatenate([rot(x_row), rot(x_col)], axis=-1) * sin[:, None]


def _window_sink_attention(q, k, v, kc, vc, sinks):
    B, T = q.shape[:2]
    n = T // ATT_BLOCK
    scale = HEAD_DIM ** -0.5
    qb = q.reshape(B, n, ATT_BLOCK, KV_HEADS, GQ, HEAD_DIM).transpose(1, 0, 2, 3, 4, 5)
    pad = ((0, 0), (WINDOW, WINDOW), (0, 0), (0, 0))
    k_pad, v_pad = jnp.pad(k, pad), jnp.pad(v, pad)
    sink_logit = sinks.astype(jnp.float32).reshape(1, KV_HEADS, GQ, 1, 1)
    rel = jnp.arange(BAND)[None, :] - jnp.arange(ATT_BLOCK)[:, None]
    in_window = (rel >= 0) & (rel <= 2 * WINDOW)

    def block(args):
        qi, idx = args
        start = idx * ATT_BLOCK
        kb = lax.dynamic_slice_in_dim(k_pad, start, BAND, axis=1)
        vb = lax.dynamic_slice_in_dim(v_pad, start, BAND, axis=1)
        key_pos = start - WINDOW + jnp.arange(BAND)
        valid = in_window & ((key_pos >= 0) & (key_pos < T))[None, :]
        s_loc = jnp.einsum('bihgd,bjhd->bhgij', qi, kb).astype(jnp.float32) * scale
        s_loc = jnp.where(valid, s_loc, NEG_INF)
        s_ctx = jnp.einsum('bihgd,bjhd->bhgij', qi, kc).astype(jnp.float32) * scale
        sink = jnp.broadcast_to(sink_logit, s_loc.shape[:-1] + (1,))
        p = jax.nn.softmax(jnp.concatenate([s_loc, s_ctx, sink], axis=-1), axis=-1).astype(v.dtype)
        L = kc.shape[1]
        return (jnp.einsum('bhgij,bjhd->bihgd', p[..., :BAND], vb)
                + jnp.einsum('bhgij,bjhd->bihgd', p[..., BAND:BAND + L], vc))

    o = lax.map(block, (qb, jnp.arange(n)))
    return o.transpose(1, 0, 2, 3, 4, 5).reshape(B, T, Q_W)


def _ec_moe(h, logits, w_gate, w_up, w_down):
    B, T, D = h.shape
    cap = EC_CAPACITY * T // N_EXPERTS
    aff = jax.nn.softmax(logits.astype(jnp.float32), axis=-1)
    gates, idx = lax.top_k(jnp.swapaxes(aff, 1, 2), cap)
    xg = jax.vmap(lambda hb, ib: hb[ib])(h, idx)
    hid = jax.nn.silu(jnp.einsum('becd,edf->becf', xg, w_gate)) * jnp.einsum('becd,edf->becf', xg, w_up)
    y = jnp.einsum('becf,efd->becd', hid, w_down) * gates[..., None].astype(h.dtype)
    return jax.vmap(lambda yb, ib: jnp.zeros((T, D), h.dtype).at[ib.reshape(-1)].add(yb.reshape(-1, D)))(y, idx)


def kernel(x, c, ctx, c_ctx, mod_w, mod_b, norm_mix, norm_ffn, ab_w_in, gla_gate_w, gla_gate_b, gla_norm,
           sg_norm_w, sg_norm_b, sg_ws, sg_bs, ab_w_out, c_w_in, sinks, c_w_out, router_w, moe_w_gate,
           moe_w_up, moe_w_down, final_norm):
    B, T, D = x.shape
    L = ctx.shape[1]
    bf = jnp.bfloat16
    ML, MC = B * T, B * L
    cos, sin = _axial_rope_tables(T, x.dtype)
    h, hc = x, ctx

    cond = jnp.concatenate([jax.nn.silu(c), jax.nn.silu(c_ctx)[None], jnp.zeros((3, D), x.dtype)], axis=0)

    for layer in range(DEPTH):
        last = layer == DEPTH - 1
        mods = _mm(cond.astype(bf), mod_w[layer].astype(bf), 8, 2048)[:B + 1] + mod_b[layer]
        mods = mods.reshape(B + 1, N_MOD, 1, D)
        mod = lambda i: mods[:, i]
        m_l = [mods[:B, i] for i in range(N_MOD)]
        m_c = [mods[B:, i] for i in range(N_MOD)]

        hall = jnp.concatenate([h.reshape(ML, D), hc.reshape(MC, D)], axis=0)
        xall = _norm_mod(hall, norm_mix[layer], mod(0), mod(1), T, 512, bf)

        if layer % 2 == 0:
            e = layer // 2
            w_in = ab_w_in[e]
            w_main = jnp.concatenate([w_in[:, :6144], w_in[:, 6176:]], axis=1).astype(bf)
            w_lr = jnp.pad(w_in[:, 6144:6176], ((0, 0), (0, 96))).astype(bf)
            p = _mm(xall, w_main, 1024, 1024)
            lr = _mm(xall, w_lr, 1024, 128)[:, :32]
            gla_l, g_l, u_l, vs_l = _ab_split(p[:ML].reshape(B, T, -1), lr[:ML].reshape(B, T, -1),
                                              gla_gate_w[e], gla_gate_b[e])
            gla_c, g_c, u_c, vs_c = _ab_split(p[ML:].reshape(B, L, -1), lr[ML:].reshape(B, L, -1),
                                              gla_gate_w[e], gla_gate_b[e])
            s0 = jnp.zeros((B, GLA_HEADS, GLA_DK, GLA_DV), jnp.float32)
            o_c, s_f, s_b = _gla_bidir(*gla_c, s0, s0)
            o_l, _, _ = _gla_bidir(*gla_l, s_f, s_b)

            def merge(o, g, u, vs):
                a = _gla_output(o, g, gla_norm[e])
                b = _spatial_gate(u, vs, sg_norm_w[e], sg_norm_b[e], sg_ws[e], sg_bs[e])
                return jnp.concatenate([a, b], axis=-1)

            mix = jnp.concatenate([merge(o_l, g_l, u_l, vs_l).reshape(ML, D),
                                   merge(o_c, g_c, u_c, vs_c).reshape(MC, D)], axis=0)
            y = _mm(mix.astype(bf), ab_w_out[e].astype(bf), 1024, 1024)
            y_l, y_c = y[:ML].reshape(B, T, D), y[ML:].reshape(B, L, D)
        else:
            o = layer // 2
            p = _mm(xall, c_w_in[o].astype(bf), 1024, 1024)
            q, k, v = _split_cols(p[:ML].reshape(B, T, -1), (Q_W, KV_W, KV_W))
            q = _apply_axial_rope(q.reshape(B, T, ATT_HEADS, HEAD_DIM), cos, sin)
            k = _apply_axial_rope(k.reshape(B, T, KV_HEADS, HEAD_DIM), cos, sin)
            v = v.reshape(B, T, KV_HEADS, HEAD_DIM)
            kc, vc = _split_cols(p[ML:, Q_W:].reshape(B, L, -1), (KV_W, KV_W))
            kc = kc.reshape(B, L, KV_HEADS, HEAD_DIM)
            vc = vc.reshape(B, L, KV_HEADS, HEAD_DIM)
            att = _window_sink_attention(q, k, v, kc, vc, sinks[o])
            y_l = _mm(att.reshape(ML, Q_W).astype(bf), c_w_out[o].astype(bf), 1024, 1024).reshape(B, T, D)
            y_c = None

        moe = (moe_w_gate[layer], moe_w_up[layer], moe_w_down[layer])
        rw = jnp.pad(router_w[layer], ((0, 0), (0, 128 - N_EXPERTS))).astype(bf)
        h = h + m_l[2] * y_l
        hm = _norm_mod(h.reshape(ML, D), norm_ffn[layer], mods[:B, 3], mods[:B, 4], T, 512, jnp.float32)
        logits = _mm(hm.astype(bf), rw, 1024, 128)[:, :N_EXPERTS]
        h = h + m_l[5] * _ec_moe(hm.reshape(B, T, D), logits.reshape(B, T, N_EXPERTS), *moe)
        if not last:
            hc = hc + m_c[2] * y_c
            hcm = _norm_mod(hc.reshape(MC, D), norm_ffn[layer], mods[B:, 3], mods[B:, 4], MC, 512, jnp.float32)
            lc = _mm(hcm.astype(bf), rw, 1024, 128)[:, :N_EXPERTS]
            hc = hc + m_c[5] * _ec_moe(hcm.reshape(B, L, D), lc.reshape(B, L, N_EXPERTS), *moe)

    zero = jnp.zeros((1, 1, D), jnp.float32)
    out = _norm_mod(h.reshape(ML, D), final_norm, zero, zero, ML, 512, jnp.float32)
    return out.reshape(B, T, D)
```

```python
import functools

import jax
import jax.numpy as jnp
import numpy as np
from jax import lax
from jax.experimental import pallas as pl
from jax.experimental.pallas import tpu as pltpu

D_MODEL = 4096
DEPTH = 2
GRID_W = 64
EPS = 1e-6
NEG_INF = -1e30
N_MOD = 6
MIX_W = D_MODEL
GLA_HEADS = 4
GLA_V_W = MIX_W // 2
GLA_DV = GLA_V_W // GLA_HEADS
GLA_K_W = GLA_V_W // 2
GLA_DK = GLA_K_W // GLA_HEADS
GLA_RANK = 16
GLA_TAU = 16.0
GLA_CHUNK = 64
SG_GROUPS = 4
SG_W = MIX_W // 2
SG_GC = SG_W // SG_GROUPS
SG_CHUNK = 128
ATT_HEADS = D_MODEL // 128
KV_HEADS = 8
HEAD_DIM = 128
GQ = ATT_HEADS // KV_HEADS
WINDOW = 128
ATT_BLOCK = 128
AXIS_DIM = HEAD_DIM // 2
ROPE_BASE = 10000.0
Q_W = ATT_HEADS * HEAD_DIM
KV_W = KV_HEADS * HEAD_DIM
N_EXPERTS = 16
EC_CAPACITY = 2

V7X_VMEM_LIMIT_BYTES = 56 * 1024 * 1024
LANES = 128

BF = jnp.bfloat16
F32 = jnp.float32


def _cparams(*sem):
    return pltpu.CompilerParams(dimension_semantics=sem, vmem_limit_bytes=V7X_VMEM_LIMIT_BYTES)


def _mm_body(x_ref, w_ref, o_ref):
    o_ref[...] = jnp.dot(x_ref[...], w_ref[...], preferred_element_type=F32).astype(o_ref.dtype)


def _mm(x, w, tm, tn, out_dtype=F32):
    m, k = x.shape
    n = w.shape[1]
    assert m % tm == 0 and n % tn == 0, (m, n, tm, tn)
    return pl.pallas_call(
        _mm_body,
        grid=(m // tm, n // tn),
        in_specs=[pl.BlockSpec((tm, k), lambda i, j: (i, 0)),
                  pl.BlockSpec((k, tn), lambda i, j: (0, j))],
        out_specs=pl.BlockSpec((tm, tn), lambda i, j: (i, j)),
        out_shape=jax.ShapeDtypeStruct((m, n), out_dtype),
        compiler_params=_cparams("parallel", "arbitrary"),
        name="mm",
    )(x, w)


def _mm_res_body(x_ref, w_ref, r_ref, g_ref, o_ref):
    acc = jnp.dot(x_ref[...], w_ref[...], preferred_element_type=F32)
    o_ref[...] = r_ref[...] + g_ref[0] * acc


def _mm_res(x, w, res, gate, rows_per_group, tm, tn):
    m, k = x.shape
    n = w.shape[1]
    g = gate.shape[0]
    per = rows_per_group // tm
    assert m % tm == 0 and n % tn == 0 and rows_per_group % tm == 0
    return pl.pallas_call(
        _mm_res_body,
        grid=(m // tm, n // tn),
        in_specs=[pl.BlockSpec((tm, k), lambda i, j: (i, 0)),
                  pl.BlockSpec((k, tn), lambda i, j: (0, j)),
                  pl.BlockSpec((tm, tn), lambda i, j: (i, j)),
                  pl.BlockSpec((1, 1, tn), lambda i, j: (jnp.minimum(i // per, g - 1), 0, j))],
        out_specs=pl.BlockSpec((tm, tn), lambda i, j: (i, j)),
        out_shape=jax.ShapeDtypeStruct((m, n), F32),
        compiler_params=_cparams("parallel", "arbitrary"),
        name="mm_res",
    )(x, w, res, gate)


def _modulated(x, gain, shift, scale):
    y = x * lax.rsqrt(jnp.mean(x * x, axis=-1, keepdims=True) + EPS)
    return (y * gain) * (1.0 + scale) + shift


def _norm_mod_body(x_ref, g_ref, sh_ref, sc_ref, o_ref):
    o_ref[...] = _modulated(x_ref[...], g_ref[...], sh_ref[0], sc_ref[0]).astype(o_ref.dtype)


def _group_index(rows_per_group, tm, groups):
    per = rows_per_group // tm
    assert rows_per_group % tm == 0
    return lambda i: (jnp.minimum(i // per, groups - 1), 0, 0)


def _norm_mod(h, gain, shift, scale, rows_per_group, tm, out_dtype):
    m, d = h.shape
    grp = _group_index(rows_per_group, tm, shift.shape[0])
    return pl.pallas_call(
        _norm_mod_body,
        grid=(m // tm,),
        in_specs=[pl.BlockSpec((tm, d), lambda i: (i, 0)),
                  pl.BlockSpec((1, d), lambda i: (0, 0)),
                  pl.BlockSpec((1, 1, d), grp),
                  pl.BlockSpec((1, 1, d), grp)],
        out_specs=pl.BlockSpec((tm, d), lambda i: (i, 0)),
        out_shape=jax.ShapeDtypeStruct((m, d), out_dtype),
        compiler_params=_cparams("parallel"),
        name="norm_mod",
    )(h, gain.reshape(1, d), shift, scale)


def _router_body(x_ref, g_ref, sh_ref, sc_ref, rwt_ref, hm_ref, aff_ref):
    hm = _modulated(x_ref[...], g_ref[...], sh_ref[0], sc_ref[0])
    hm_ref[...] = hm
    logits = lax.dot_general(rwt_ref[...], hm.astype(BF), (((1,), (1,)), ((), ())),
                             preferred_element_type=F32)
    e = jnp.exp(logits - jnp.max(logits, axis=0, keepdims=True))
    aff_ref[...] = e / jnp.sum(e, axis=0, keepdims=True)


def _router(h, gain, shift, scale, rw_t, rows_per_group, tm):
    m, d = h.shape
    ne = rw_t.shape[0]
    grp = _group_index(rows_per_group, tm, shift.shape[0])
    return pl.pallas_call(
        _router_body,
        grid=(m // tm,),
        in_specs=[pl.BlockSpec((tm, d), lambda i: (i, 0)),
                  pl.BlockSpec((1, d), lambda i: (0, 0)),
                  pl.BlockSpec((1, 1, d), grp),
                  pl.BlockSpec((1, 1, d), grp),
                  pl.BlockSpec((ne, d), lambda i: (0, 0))],
        out_specs=[pl.BlockSpec((tm, d), lambda i: (i, 0)),
                   pl.BlockSpec((ne, tm), lambda i: (0, i))],
        out_shape=[jax.ShapeDtypeStruct((m, d), F32), jax.ShapeDtypeStruct((ne, m), F32)],
        compiler_params=_cparams("parallel"),
        name="router",
    )(h, gain.reshape(1, d), shift, scale, rw_t)


def _experts_body(idx_vm, gates_ref, hm_hbm, wg_ref, wu_ref, wd_ref, y_ref,
                  idx_sm, xg, hid, stage, sem, *, rows, gather_rows, mm_rows, n_ff_tiles):
    j = pl.program_id(1)
    n_chunks = rows // gather_rows
    tf = hid.shape[2]

    def row_copy(c, r, slot):
        src = idx_sm[0, c * gather_rows + r]
        return pltpu.make_async_copy(hm_hbm.at[pl.ds(src, 1)], stage.at[slot, pl.ds(r, 1)], sem.at[slot])

    def issue(c, slot):
        def body(r, carry):
            row_copy(c, r, slot).start()
            return carry
        lax.fori_loop(0, gather_rows, body, 0)

    def drain(c, slot):
        def body(r, carry):
            row_copy(c, r, slot).wait()
            return carry
        lax.fori_loop(0, gather_rows, body, 0)

    @pl.when(j == 0)
    def _gather():
        pltpu.sync_copy(idx_vm.at[0], idx_sm)
        issue(0, 0)
        for c in range(n_chunks):
            if c + 1 < n_chunks:
                issue(c + 1, (c + 1) % 2)
            drain(c, c % 2)
            xg[c * gather_rows:(c + 1) * gather_rows, :] = stage[c % 2].astype(BF)

    @pl.when(j < n_ff_tiles)
    def _up():
        for rc in range(rows // mm_rows):
            rs = slice(rc * mm_rows, (rc + 1) * mm_rows)
            xs = xg[rs, :]
            g = jnp.dot(xs, wg_ref[0], preferred_element_type=F32)
            u = jnp.dot(xs, wu_ref[0], preferred_element_type=F32)
            hid[j, rs, :] = ((g * jax.nn.sigmoid(g)) * u).astype(BF)

    @pl.when(j >= n_ff_tiles)
    def _down():
        for rc in range(rows // mm_rows):
            rs = slice(rc * mm_rows, (rc + 1) * mm_rows)
            acc = jnp.dot(hid[0, rs, :], wd_ref[0, 0:tf, :], preferred_element_type=F32)
            for f in range(1, n_ff_tiles):
                acc = acc + jnp.dot(hid[f, rs, :], wd_ref[0, f * tf:(f + 1) * tf, :], preferred_element_type=F32)
            y_ref[0, rs, :] = acc * gates_ref[0, rs, :]


def _experts(hm, idx, gates, wg, wu, wd, tf=256, td=512):
    ne, _, rows = idx.shape
    d = hm.shape[1]
    ff = wg.shape[2]
    n_ff_tiles, n_d_tiles = ff // tf, d // td
    gather_rows = rows // 16
    mm_rows = rows // 4
    assert rows % 128 == 0 and gather_rows % 8 == 0
    body = functools.partial(_experts_body, rows=rows, gather_rows=gather_rows, mm_rows=mm_rows,
                             n_ff_tiles=n_ff_tiles)
    last_ff = n_ff_tiles - 1
    return pl.pallas_call(
        body,
        grid=(ne, n_ff_tiles + n_d_tiles),
        in_specs=[pl.BlockSpec((1, 1, rows), lambda e, j: (e, 0, 0)),
                  pl.BlockSpec((1, rows, 1), lambda e, j: (e, 0, 0)),
                  pl.BlockSpec(memory_space=pl.ANY),
                  pl.BlockSpec((1, d, tf), lambda e, j: (e, 0, jnp.minimum(j, last_ff))),
                  pl.BlockSpec((1, d, tf), lambda e, j: (e, 0, jnp.minimum(j, last_ff))),
                  pl.BlockSpec((1, ff, td), lambda e, j: (e, 0, jnp.maximum(j - n_ff_tiles, 0)))],
        out_specs=pl.BlockSpec((1, rows, td), lambda e, j: (e, 0, jnp.maximum(j - n_ff_tiles, 0))),
        out_shape=jax.ShapeDtypeStruct((ne, rows, d), F32),
        scratch_shapes=[pltpu.SMEM((1, rows), jnp.int32),
                        pltpu.VMEM((rows, d), BF),
                        pltpu.VMEM((n_ff_tiles, rows, tf), BF),
                        pltpu.VMEM((2, gather_rows, d), F32),
                        pltpu.SemaphoreType.DMA((2,))],
        compiler_params=_cparams("arbitrary", "arbitrary"),
        name="experts",
    )(idx, gates, hm, wg, wu, wd)


COMBINE_TOKENS = 64
COMBINE_STRIP = 512


def _combine_body(pk_ref, ts_ref, mc_ref, h_ref, cnt_ref, gate_ref, y_hbm, o_ref, buf, sem, *, n_tiles, n_slots):
    g = pl.program_id(0)
    tt, d = h_ref.shape

    def row_copy(p, slot):
        v = pk_ref[p]
        src = jnp.bitwise_and(v, 0xFFFF)
        dst = jnp.right_shift(v, 16)
        return pltpu.make_async_copy(y_hbm.at[pl.ds(src, 1)], buf.at[slot, pl.ds(dst, 1)], sem.at[slot])

    def issue(t, slot):
        def body(p, carry):
            row_copy(p, slot).start()
            return carry
        lax.fori_loop(ts_ref[t], ts_ref[t + 1], body, 0)

    def drain(t, slot):
        def body(p, carry):
            row_copy(p, slot).wait()
            return carry
        lax.fori_loop(ts_ref[t], ts_ref[t + 1], body, 0)

    @pl.when(g == 0)
    def _first():
        def zero(r, carry):
            buf[:, pl.ds(pl.multiple_of(r * tt, tt), tt), :] = jnp.zeros((2, tt, d), F32)
            return carry
        lax.fori_loop(0, n_slots, zero, 0)
        issue(0, 0)

    @pl.when(g + 1 < n_tiles)
    def _prefetch():
        issue(g + 1, (g + 1) % 2)

    slot = g % 2
    drain(g, slot)
    n_used = mc_ref[g]
    cnt = cnt_ref[...]
    for cs in range(d // COMBINE_STRIP):
        cols = slice(cs * COMBINE_STRIP, (cs + 1) * COMBINE_STRIP)

        def body(k, acc):
            v = buf[slot, pl.ds(pl.multiple_of(k * tt, tt), tt), cols]
            return acc + jnp.where(cnt > k, v, 0.0)

        acc = lax.fori_loop(0, n_used, body, jnp.zeros((tt, COMBINE_STRIP), F32))
        o_ref[:, cols] = h_ref[:, cols] + gate_ref[0][:, cols] * acc


def _combine(h, y2d, gate, tok, yrow, row_off, n_tok, rows_per_group, n_slots):
    tt = COMBINE_TOKENS
    d = h.shape[1]
    n_pairs = tok.shape[0]
    n_tiles = n_tok // tt
    assert n_tok % tt == 0 and row_off % tt == 0 and y2d.shape[0] <= 0x10000 and n_slots * tt <= 0x7FFF
    order = jnp.argsort(tok)
    tok_s = tok[order]
    start = jnp.searchsorted(tok_s, jnp.arange(n_tok + 1, dtype=jnp.int32), side='left').astype(jnp.int32)
    cnt = start[1:] - start[:-1]
    slot = jnp.arange(n_pairs, dtype=jnp.int32) - start[tok_s]
    dst = slot * tt + tok_s % tt
    packed = jnp.bitwise_or(yrow[order].astype(jnp.int32), jnp.left_shift(dst, 16))
    tile_start = start[::tt]
    tile_slots = jnp.max(cnt.reshape(n_tiles, tt), axis=1)
    tile0 = row_off // tt
    per = rows_per_group // tt
    n_groups = gate.shape[0]
    body = functools.partial(_combine_body, n_tiles=n_tiles, n_slots=n_slots)
    grid_spec = pltpu.PrefetchScalarGridSpec(
        num_scalar_prefetch=3,
        grid=(n_tiles,),
        in_specs=[pl.BlockSpec((tt, d), lambda g, *_: (tile0 + g, 0)),
                  pl.BlockSpec((tt, 1), lambda g, *_: (g, 0)),
                  pl.BlockSpec((1, 1, d), lambda g, *_: (jnp.minimum(g // per, n_groups - 1), 0, 0)),
                  pl.BlockSpec(memory_space=pl.ANY)],
        out_specs=pl.BlockSpec((tt, d), lambda g, *_: (tile0 + g, 0)),
        scratch_shapes=[pltpu.VMEM((2, n_slots * tt, d), F32),
                        pltpu.SemaphoreType.DMA((2,))],
    )
    return pl.pallas_call(
        body,
        grid_spec=grid_spec,
        out_shape=jax.ShapeDtypeStruct(h.shape, F32),
        input_output_aliases={3: 0},
        compiler_params=_cparams("arbitrary"),
        name="combine",
    )(packed, tile_start, tile_slots, h, cnt.reshape(n_tok, 1), gate, y2d)


ATT_Q_TILE = 512


def _rope(x, cos, sin):
    lane = lax.broadcasted_iota(jnp.int32, x.shape, 1)
    fwd = pltpu.roll(x, AXIS_DIM // 2, axis=1)
    bwd = pltpu.roll(x, HEAD_DIM - AXIS_DIM // 2, axis=1)
    rot = jnp.where(lane % AXIS_DIM < AXIS_DIM // 2, -bwd, fwd)
    return x * cos + rot * sin


def _attn_body(q_ref, kp_ref, km_ref, kn_ref, vp_ref, vm_ref, vn_ref, kc_ref, vc_ref,
               cm_ref, sm_ref, cp_ref, sp_ref, cn_ref, sn_ref, sink_ref, o_ref, *, n_q_tiles):
    i = pl.program_id(2)
    qt = q_ref.shape[0]
    blk = ATT_BLOCK
    nb = qt // blk
    scale = HEAD_DIM ** -0.5
    cos_w = jnp.concatenate([cp_ref[...], cm_ref[...], cn_ref[...]], axis=0)
    sin_w = jnp.concatenate([sp_ref[...], sm_ref[...], sn_ref[...]], axis=0)
    k_w = jnp.concatenate([kp_ref[...], km_ref[...], kn_ref[...]], axis=0)
    k_w = _rope(k_w, cos_w, sin_w).astype(BF)
    v_w = jnp.concatenate([vp_ref[...], vm_ref[...], vn_ref[...]], axis=0).astype(BF)
    kc = kc_ref[...].astype(BF)
    vc = vc_ref[...].astype(BF)
    sink = sink_ref[0]
    nt = (((1,), (1,)), ((), ()))
    band = 3 * blk
    row = lax.broadcasted_iota(jnp.int32, (GQ * blk, band), 0) % blk
    col = lax.broadcasted_iota(jnp.int32, (GQ * blk, band), 1)
    in_window = (col >= row) & (col <= row + 2 * WINDOW)
    for jb in range(nb):
        rs = slice(jb * blk, (jb + 1) * blk)
        cos_q = cm_ref[rs, :]
        sin_q = sm_ref[rs, :]
        qs = jnp.concatenate([_rope(q_ref[rs, g * HEAD_DIM:(g + 1) * HEAD_DIM], cos_q, sin_q)
                              for g in range(GQ)], axis=0).astype(BF)
        kb = k_w[jb * blk:jb * blk + band]
        vb = v_w[jb * blk:jb * blk + band]
        s_loc = lax.dot_general(qs, kb, nt, preferred_element_type=F32) * scale
        s_ctx = lax.dot_general(qs, kc, nt, preferred_element_type=F32) * scale
        valid = in_window
        if jb == 0:
            valid = valid & (col >= jnp.where(i == 0, blk, 0))
        if jb == nb - 1:
            valid = valid & (col < jnp.where(i == n_q_tiles - 1, 2 * blk, band))
        s_loc = jnp.where(valid, s_loc, NEG_INF)
        m = jnp.maximum(jnp.maximum(jnp.max(s_loc, axis=1, keepdims=True),
                                    jnp.max(s_ctx, axis=1, keepdims=True)), sink)
        e_loc = jnp.exp(s_loc - m)
        e_ctx = jnp.exp(s_ctx - m)
        den = (jnp.sum(e_loc, axis=1, keepdims=True) + jnp.sum(e_ctx, axis=1, keepdims=True)
               + jnp.exp(sink - m))
        o = (jnp.dot(e_loc.astype(BF), vb, preferred_element_type=F32)
             + jnp.dot(e_ctx.astype(BF), vc, preferred_element_type=F32)) / den
        for g in range(GQ):
            o_ref[rs, g * HEAD_DIM:(g + 1) * HEAD_DIM] = o[g * blk:(g + 1) * blk].astype(o_ref.dtype)


def _attention(p, cos, sin, sinks, n_batch, seq, ctx_len):
    qt, blk = ATT_Q_TILE, ATT_BLOCK
    ml = n_batch * seq
    nq = seq // qt
    nbk = seq // blk
    bpt = qt // blk
    kcol, vcol = Q_W // HEAD_DIM, (Q_W + KV_W) // HEAD_DIM
    assert seq % qt == 0 and ml % ctx_len == 0
    sink_col = jnp.repeat(sinks.astype(F32).reshape(KV_HEADS, GQ, 1), blk, axis=2).reshape(KV_HEADS, GQ * blk, 1)
    prev_blk = lambda i: jnp.maximum(i * bpt - 1, 0)
    next_blk = lambda i: jnp.minimum((i + 1) * bpt, nbk - 1)
    qspec = pl.BlockSpec((qt, GQ * HEAD_DIM), lambda b, h, i: (b * nq + i, h))

    def kv_specs(c0):
        return [pl.BlockSpec((blk, HEAD_DIM), lambda b, h, i: (b * nbk + prev_blk(i), c0 + h)),
                pl.BlockSpec((qt, HEAD_DIM), lambda b, h, i: (b * nq + i, c0 + h)),
                pl.BlockSpec((blk, HEAD_DIM), lambda b, h, i: (b * nbk + next_blk(i), c0 + h))]

    ctx_spec = lambda c0: pl.BlockSpec((ctx_len, HEAD_DIM), lambda b, h, i: (ml // ctx_len + b, c0 + h))
    tab_specs = [pl.BlockSpec((qt, HEAD_DIM), lambda b, h, i: (i, 0)),
                 pl.BlockSpec((qt, HEAD_DIM), lambda b, h, i: (i, 0)),
                 pl.BlockSpec((blk, HEAD_DIM), lambda b, h, i: (prev_blk(i), 0)),
                 pl.BlockSpec((blk, HEAD_DIM), lambda b, h, i: (prev_blk(i), 0)),
                 pl.BlockSpec((blk, HEAD_DIM), lambda b, h, i: (next_blk(i), 0)),
                 pl.BlockSpec((blk, HEAD_DIM), lambda b, h, i: (next_blk(i), 0))]
    return pl.pallas_call(
        functools.partial(_attn_body, n_q_tiles=nq),
        grid=(n_batch, KV_HEADS, nq),
        in_specs=[qspec] + kv_specs(kcol) + kv_specs(vcol) + [ctx_spec(kcol), ctx_spec(vcol)] + tab_specs
                 + [pl.BlockSpec((1, GQ * blk, 1), lambda b, h, i: (h, 0, 0))],
        out_specs=pl.BlockSpec((qt, GQ * HEAD_DIM), lambda b, h, i: (b * nq + i, h)),
        out_shape=jax.ShapeDtypeStruct((ml, Q_W), BF),
        compiler_params=_cparams("parallel", "parallel", "arbitrary"),
        name="attention",
    )(p, p, p, p, p, p, p, p, p, cos, sin, cos, sin, cos, sin, sink_col)


def _split_cols(p, widths):
    return jnp.split(p, [int(i) for i in np.cumsum(widths)[:-1]], axis=-1)


def _gla_chunked(q, k, v, log_a, s0):
    B, T, H, dk = q.shape
    dv = v.shape[-1]
    n = T // GLA_CHUNK
    shp = lambda a: a.astype(jnp.float32).reshape(B, n, GLA_CHUNK, H, a.shape[-1])
    q, k, v, log_a = shp(q), shp(k), shp(v), shp(log_a)
    b = jnp.cumsum(log_a, axis=2)
    b_last = b[:, :, -1:]
    q_in = q * jnp.exp(b)
    k_in = k * jnp.exp(-b)
    k_dec = k * jnp.exp(b_last - b)
    causal = jnp.tril(jnp.ones((GLA_CHUNK, GLA_CHUNK), bool))
    scores = jnp.where(causal, jnp.einsum('bnihd,bnjhd->bnhij', q_in, k_in), 0.0)
    o_intra = jnp.einsum('bnhij,bnjhe->bnihe', scores, v)
    decay = jnp.exp(b_last[:, :, 0])

    def step(S, xs):
        qc, kc, vc, dc = xs
        o = jnp.einsum('bihd,bhde->bihe', qc, S)
        S = dc[..., None] * S + jnp.einsum('bjhd,bjhe->bhde', kc, vc)
        return S, o

    xs = tuple(jnp.moveaxis(a, 1, 0) for a in (q_in, k_dec, v, decay))
    S, o_inter = lax.scan(step, s0.astype(jnp.float32), xs)
    o = o_intra + jnp.moveaxis(o_inter, 0, 1)
    return o.reshape(B, T, H, dv), S


def _gla_bidir(q, k, v, la_f, la_b, s_f, s_b):
    rev = lambda a: jnp.flip(a, axis=1)
    o_f, s_f = _gla_chunked(q, k, v, la_f, s_f)
    o_b, s_b = _gla_chunked(rev(q), rev(k), rev(v), rev(la_b), s_b)
    return o_f + rev(o_b), s_f, s_b


def _gla_output(o, g, gain):
    B, T = o.shape[:2]
    y = o * lax.rsqrt(jnp.mean(o * o, axis=-1, keepdims=True) + EPS)
    y = y * gain.astype(jnp.float32).reshape(GLA_HEADS, GLA_DV)
    return y.reshape(B, T, GLA_V_W).astype(g.dtype) * jax.nn.silu(g)


def _spatial_gate(u, vs, nw, nb, ws, bs):
    B, T, _ = u.shape
    u = jax.nn.gelu(u)
    vf = jax.nn.gelu(vs).astype(jnp.float32).reshape(B, T, SG_GROUPS, SG_GC)
    mu = jnp.mean(vf, axis=-1, keepdims=True)
    var = jnp.mean(jnp.square(vf - mu), axis=-1, keepdims=True)
    vn = (vf - mu) * lax.rsqrt(var + EPS) * nw.reshape(SG_GROUPS, SG_GC) + nb.reshape(SG_GROUPS, SG_GC)
    vn = vn.reshape(B, T // SG_CHUNK, SG_CHUNK, SG_GROUPS, SG_GC)
    s = jnp.einsum('gij,bnjgc->bnigc', ws, vn) + bs.T[:, :, None]
    return u * s.reshape(B, T, SG_W)


def _ab_split(p, lr, gate_w, gate_b):
    B, T, _ = p.shape
    q, k, v, g, u, vs = _split_cols(p, (GLA_K_W, GLA_K_W, GLA_V_W, GLA_V_W, SG_W, SG_W))
    lr_f, lr_b = lr[..., :GLA_RANK], lr[..., GLA_RANK:2 * GLA_RANK]
    heads = lambda a, d: a.reshape(B, T, GLA_HEADS, d)
    log_decay = lambda l, d: jax.nn.log_sigmoid((l @ gate_w[d] + gate_b[d]).astype(jnp.float32)) / GLA_TAU
    gla_in = (heads(q, GLA_DK) * GLA_DK ** -0.5, heads(k, GLA_DK), heads(v, GLA_DV),
              heads(log_decay(lr_f, 0), GLA_DK), heads(log_decay(lr_b, 1), GLA_DK))
    return gla_in, g, u, vs


def _axial_rope_tables(T):
    rows = T // GRID_W
    row, col = jnp.meshgrid(jnp.arange(rows), jnp.arange(GRID_W), indexing='ij')
    inv_freq = ROPE_BASE ** (-jnp.arange(0, AXIS_DIM, 2, dtype=jnp.float32) / AXIS_DIM)

    def angles(pos):
        ang = pos.reshape(-1).astype(jnp.float32)[:, None] * inv_freq
        return jnp.concatenate([ang, ang], axis=-1)

    ang = jnp.concatenate([angles(row), angles(col)], axis=-1)
    return jnp.cos(ang), jnp.sin(ang)


def _moe(hall, groups, gain, shift, scale, gate, rw_t, wg, wu, wd):
    m, d = hall.shape
    ne = rw_t.shape[0]
    set_len0 = groups[0][2]
    hm, aff = _router(hall, gain, shift, scale, rw_t, set_len0, 512)
    idx_parts, gate_parts, meta = [], [], []
    r_off = 0
    for row_off, n_sets, set_len in groups:
        cap = EC_CAPACITY * set_len // ne
        a = aff[:, row_off:row_off + n_sets * set_len].reshape(ne, n_sets, set_len)
        gts, idx = lax.top_k(a, cap)
        rows = idx + (row_off + jnp.arange(n_sets, dtype=jnp.int32) * set_len)[None, :, None]
        idx_parts.append(rows.reshape(ne, n_sets * cap))
        gate_parts.append(gts.reshape(ne, n_sets * cap))
        meta.append((row_off, n_sets, set_len, cap, r_off, idx))
        r_off += n_sets * cap
    rows_per_expert = r_off
    idx_all = jnp.concatenate(idx_parts, axis=1).astype(jnp.int32)
    gates_all = jnp.concatenate(gate_parts, axis=1)
    y = _experts(hm[:m], idx_all.reshape(ne, 1, rows_per_expert), gates_all.reshape(ne, rows_per_expert, 1),
                 wg, wu, wd)
    y2d = y.reshape(ne * rows_per_expert, d)
    e_base = (jnp.arange(ne, dtype=jnp.int32) * rows_per_expert)[:, None]
    for gi, (row_off, n_sets, set_len, cap, r0, idx) in enumerate(meta):
        if gi == 0:
            for s in range(n_sets):
                tok = idx[:, s, :].reshape(-1).astype(jnp.int32)
                yrow = (e_base + r0 + s * cap + jnp.arange(cap, dtype=jnp.int32)[None, :]).reshape(-1)
                hall = _combine(hall, y2d, gate[s:s + 1], tok, yrow, row_off + s * set_len, set_len, set_len, ne)
        else:
            tok = (idx + (jnp.arange(n_sets, dtype=jnp.int32) * set_len)[None, :, None]).reshape(-1).astype(jnp.int32)
            yrow = (e_base + r0 + jnp.arange(n_sets * cap, dtype=jnp.int32)[None, :]).reshape(-1)
            hall = _combine(hall, y2d, gate[-1:], tok, yrow, row_off, n_sets * set_len, n_sets * set_len, ne)
    return hall


def kernel(x, c, ctx, c_ctx, mod_w, mod_b, norm_mix, norm_ffn, ab_w_in, gla_gate_w, gla_gate_b, gla_norm,
           sg_norm_w, sg_norm_b, sg_ws, sg_bs, ab_w_out, c_w_in, sinks, c_w_out, router_w, moe_w_gate,
           moe_w_up, moe_w_down, final_norm):
    B, T, D = x.shape
    L = ctx.shape[1]
    ML, MC = B * T, B * L
    cos, sin = _axial_rope_tables(T)

    cond = jnp.concatenate([jax.nn.silu(c), jax.nn.silu(c_ctx)[None], jnp.zeros((3, D), x.dtype)], axis=0)
    hall = jnp.concatenate([x.reshape(ML, D), ctx.reshape(MC, D)], axis=0)

    for layer in range(DEPTH):
        last = layer == DEPTH - 1
        mods = _mm(cond.astype(BF), mod_w[layer].astype(BF), 8, 2048)[:B + 1] + mod_b[layer]
        mods = mods.reshape(B + 1, N_MOD, 1, D)
        mod = lambda i: mods[:, i]

        xall = _norm_mod(hall, norm_mix[layer], mod(0), mod(1), T, 512, BF)

        if layer % 2 == 0:
            e = layer // 2
            w_in = ab_w_in[e]
            w_main = jnp.concatenate([w_in[:, :6144], w_in[:, 6176:]], axis=1).astype(BF)
            w_lr = jnp.pad(w_in[:, 6144:6176], ((0, 0), (0, 96))).astype(BF)
            p = _mm(xall, w_main, 1024, 1024)
            lr = _mm(xall, w_lr, 1024, 128)[:, :32]
            gla_l, g_l, u_l, vs_l = _ab_split(p[:ML].reshape(B, T, -1), lr[:ML].reshape(B, T, -1),
                                              gla_gate_w[e], gla_gate_b[e])
            gla_c, g_c, u_c, vs_c = _ab_split(p[ML:].reshape(B, L, -1), lr[ML:].reshape(B, L, -1),
                                              gla_gate_w[e], gla_gate_b[e])
            s0 = jnp.zeros((B, GLA_HEADS, GLA_DK, GLA_DV), jnp.float32)
            o_c, s_f, s_b = _gla_bidir(*gla_c, s0, s0)
            o_l, _, _ = _gla_bidir(*gla_l, s_f, s_b)

            def merge(o, g, u, vs):
                a = _gla_output(o, g, gla_norm[e])
                b = _spatial_gate(u, vs, sg_norm_w[e], sg_norm_b[e], sg_ws[e], sg_bs[e])
                return jnp.concatenate([a, b], axis=-1)

            mix = jnp.concatenate([merge(o_l, g_l, u_l, vs_l).reshape(ML, D),
                                   merge(o_c, g_c, u_c, vs_c).reshape(MC, D)], axis=0)
            hall = _mm_res(mix.astype(BF), ab_w_out[e].astype(BF), hall, mod(2), T, 1024, 1024)
        else:
            o = layer // 2
            p = _mm(xall, c_w_in[o].astype(BF), 1024, 1024)
            att = _attention(p, cos, sin, sinks[o], B, T, L)
            hall = _mm_res(att, c_w_out[o].astype(BF), hall[:ML], mods[:B, 2], T, 1024, 1024)

        rw_t = router_w[layer].T.astype(BF)
        wg, wu, wd = moe_w_gate[layer].astype(BF), moe_w_up[layer].astype(BF), moe_w_down[layer].astype(BF)
        if last:
            hall = _moe(hall, [(0, B, T)], norm_ffn[layer], mods[:B, 3], mods[:B, 4], mods[:B, 5], rw_t, wg, wu, wd)
        else:
            hall = _moe(hall, [(0, B, T), (ML, B, L)], norm_ffn[layer], mod(3), mod(4), mod(5), rw_t, wg, wu, wd)

    zero = jnp.zeros((1, 1, D), jnp.float32)
    out = _norm_mod(hall[:ML], final_norm, zero, zero, ML, 512, jnp.float32)
    return out.reshape(B, T, D)
```

```python
import functools

import jax
import jax.numpy as jnp
import numpy as np
from jax import lax
from jax.experimental import pallas as pl
from jax.experimental.pallas import tpu as pltpu

D_MODEL = 4096
DEPTH = 2
GRID_W = 64
EPS = 1e-6
NEG_INF = -1e30
N_MOD = 6
MIX_W = D_MODEL
GLA_HEADS = 4
GLA_V_W = MIX_W // 2
GLA_DV = GLA_V_W // GLA_HEADS
GLA_K_W = GLA_V_W // 2
GLA_DK = GLA_K_W // GLA_HEADS
GLA_RANK = 16
GLA_TAU = 16.0
GLA_CHUNK = 64
SG_GROUPS = 4
SG_W = MIX_W // 2
SG_GC = SG_W // SG_GROUPS
SG_CHUNK = 128
ATT_HEADS = D_MODEL // 128
KV_HEADS = 8
HEAD_DIM = 128
GQ = ATT_HEADS // KV_HEADS
WINDOW = 128
ATT_BLOCK = 128
AXIS_DIM = HEAD_DIM // 2
ROPE_BASE = 10000.0
Q_W = ATT_HEADS * HEAD_DIM
KV_W = KV_HEADS * HEAD_DIM
N_EXPERTS = 16
EC_CAPACITY = 2

V7X_VMEM_LIMIT_BYTES = 56 * 1024 * 1024
LANES = 128

BF = jnp.bfloat16
F32 = jnp.float32


def _cparams(*sem):
    return pltpu.CompilerParams(dimension_semantics=sem, vmem_limit_bytes=V7X_VMEM_LIMIT_BYTES)


def _mm_body(x_ref, w_ref, o_ref):
    o_ref[...] = jnp.dot(x_ref[...], w_ref[...], preferred_element_type=F32).astype(o_ref.dtype)


def _mm(x, w, tm, tn, out_dtype=F32):
    m, k = x.shape
    n = w.shape[1]
    assert m % tm == 0 and n % tn == 0, (m, n, tm, tn)
    return pl.pallas_call(
        _mm_body,
        grid=(m // tm, n // tn),
        in_specs=[pl.BlockSpec((tm, k), lambda i, j: (i, 0)),
                  pl.BlockSpec((k, tn), lambda i, j: (0, j))],
        out_specs=pl.BlockSpec((tm, tn), lambda i, j: (i, j)),
        out_shape=jax.ShapeDtypeStruct((m, n), out_dtype),
        compiler_params=_cparams("parallel", "arbitrary"),
        name="mm",
    )(x, w)


def _mm_res_body(x_ref, w_ref, r_ref, g_ref, o_ref):
    acc = jnp.dot(x_ref[...], w_ref[...], preferred_element_type=F32)
    o_ref[...] = r_ref[...] + g_ref[0] * acc


def _mm_res(x, w, res, gate, rows_per_group, tm, tn):
    m, k = x.shape
    n = w.shape[1]
    g = gate.shape[0]
    per = rows_per_group // tm
    assert m % tm == 0 and n % tn == 0 and rows_per_group % tm == 0
    return pl.pallas_call(
        _mm_res_body,
        grid=(m // tm, n // tn),
        in_specs=[pl.BlockSpec((tm, k), lambda i, j: (i, 0)),
                  pl.BlockSpec((k, tn), lambda i, j: (0, j)),
                  pl.BlockSpec((tm, tn), lambda i, j: (i, j)),
                  pl.BlockSpec((1, 1, tn), lambda i, j: (jnp.minimum(i // per, g - 1), 0, j))],
        out_specs=pl.BlockSpec((tm, tn), lambda i, j: (i, j)),
        out_shape=jax.ShapeDtypeStruct((m, n), F32),
        compiler_params=_cparams("parallel", "arbitrary"),
        name="mm_res",
    )(x, w, res, gate)


def _modulated(x, gain, shift, scale):
    y = x * lax.rsqrt(jnp.mean(x * x, axis=-1, keepdims=True) + EPS)
    return (y * gain) * (1.0 + scale) + shift


def _norm_mod_body(x_ref, g_ref, sh_ref, sc_ref, o_ref):
    o_ref[...] = _modulated(x_ref[...], g_ref[...], sh_ref[0], sc_ref[0]).astype(o_ref.dtype)


def _group_index(rows_per_group, tm, groups):
    per = rows_per_group // tm
    assert rows_per_group % tm == 0
    return lambda i: (jnp.minimum(i // per, groups - 1), 0, 0)


def _norm_mod(h, gain, shift, scale, rows_per_group, tm, out_dtype):
    m, d = h.shape
    grp = _group_index(rows_per_group, tm, shift.shape[0])
    return pl.pallas_call(
        _norm_mod_body,
        grid=(m // tm,),
        in_specs=[pl.BlockSpec((tm, d), lambda i: (i, 0)),
                  pl.BlockSpec((1, d), lambda i: (0, 0)),
                  pl.BlockSpec((1, 1, d), grp),
                  pl.BlockSpec((1, 1, d), grp)],
        out_specs=pl.BlockSpec((tm, d), lambda i: (i, 0)),
        out_shape=jax.ShapeDtypeStruct((m, d), out_dtype),
        compiler_params=_cparams("parallel"),
        name="norm_mod",
    )(h, gain.reshape(1, d), shift, scale)


def _router_body(x_ref, g_ref, sh_ref, sc_ref, rwt_ref, hm_ref, aff_ref):
    hm = _modulated(x_ref[...], g_ref[...], sh_ref[0], sc_ref[0])
    for c in range(hm_ref.shape[1]):
        hm_ref[:, c, :] = hm[:, c * LANES:(c + 1) * LANES]
    logits = lax.dot_general(rwt_ref[...], hm.astype(BF), (((1,), (1,)), ((), ())),
                             preferred_element_type=F32)
    e = jnp.exp(logits - jnp.max(logits, axis=0, keepdims=True))
    aff_ref[...] = e / jnp.sum(e, axis=0, keepdims=True)


def _router(h, gain, shift, scale, rw_t, rows_per_group, tm):
    m, d = h.shape
    nc = d // LANES
    ne = rw_t.shape[0]
    grp = _group_index(rows_per_group, tm, shift.shape[0])
    return pl.pallas_call(
        _router_body,
        grid=(m // tm,),
        in_specs=[pl.BlockSpec((tm, d), lambda i: (i, 0)),
                  pl.BlockSpec((1, d), lambda i: (0, 0)),
                  pl.BlockSpec((1, 1, d), grp),
                  pl.BlockSpec((1, 1, d), grp),
                  pl.BlockSpec((ne, d), lambda i: (0, 0))],
        out_specs=[pl.BlockSpec((tm, nc, LANES), lambda i: (i, 0, 0)),
                   pl.BlockSpec((ne, tm), lambda i: (0, i))],
        out_shape=[jax.ShapeDtypeStruct((m, nc, LANES), F32), jax.ShapeDtypeStruct((ne, m), F32)],
        compiler_params=_cparams("parallel"),
        name="router",
    )(h, gain.reshape(1, d), shift, scale, rw_t)


def _experts_body(idx_vm, gates_ref, hm_hbm, wg_ref, wu_ref, wd_ref, y_ref,
                  idx_sm, xg, hid, stage, sem, *, rows, gather_rows, mm_rows, n_ff_tiles):
    j = pl.program_id(1)
    n_chunks = rows // gather_rows
    tf = hid.shape[2]
    n_lane_blocks = stage.shape[2]

    def row_copy(c, r, slot):
        src = idx_sm[0, c * gather_rows + r]
        return pltpu.make_async_copy(hm_hbm.at[src], stage.at[slot, r], sem.at[slot])

    def issue(c, slot):
        def body(r, carry):
            row_copy(c, r, slot).start()
            return carry
        lax.fori_loop(0, gather_rows, body, 0)

    def drain(c, slot):
        def body(r, carry):
            row_copy(c, r, slot).wait()
            return carry
        lax.fori_loop(0, gather_rows, body, 0)

    @pl.when(j == 0)
    def _gather():
        pltpu.sync_copy(idx_vm.at[0], idx_sm)
        issue(0, 0)
        for c in range(n_chunks):
            if c + 1 < n_chunks:
                issue(c + 1, (c + 1) % 2)
            drain(c, c % 2)
            for lb in range(n_lane_blocks):
                xg[c * gather_rows:(c + 1) * gather_rows, lb * LANES:(lb + 1) * LANES] = (
                    stage[c % 2, :, lb, :].astype(BF))

    @pl.when(j < n_ff_tiles)
    def _up():
        for rc in range(rows // mm_rows):
            rs = slice(rc * mm_rows, (rc + 1) * mm_rows)
            xs = xg[rs, :]
            g = jnp.dot(xs, wg_ref[0], preferred_element_type=F32)
            u = jnp.dot(xs, wu_ref[0], preferred_element_type=F32)
            hid[j, rs, :] = ((g * jax.nn.sigmoid(g)) * u).astype(BF)

    @pl.when(j >= n_ff_tiles)
    def _down():
        half_rows = rows // 2
        base = ((j - n_ff_tiles) % 2) * half_rows
        for rc in range(half_rows // mm_rows):
            rs = pl.ds(pl.multiple_of(base + rc * mm_rows, 8), mm_rows)
            acc = jnp.dot(hid[0, rs, :], wd_ref[0, 0:tf, :], preferred_element_type=F32)
            for f in range(1, n_ff_tiles):
                acc = acc + jnp.dot(hid[f, rs, :], wd_ref[0, f * tf:(f + 1) * tf, :], preferred_element_type=F32)
            acc = acc * gates_ref[0, rs, :]
            for lb in range(y_ref.shape[2]):
                y_ref[0, rc * mm_rows:(rc + 1) * mm_rows, lb, :] = acc[:, lb * LANES:(lb + 1) * LANES]


SUBLANES = 8


def _experts(hm, idx, gates, wg, wu, wd, tf=256):
    ne, _, rows = idx.shape
    n_lane_blocks = hm.shape[1]
    d = n_lane_blocks * LANES
    ff = wg.shape[2]
    td = SUBLANES * LANES
    n_ff_tiles, n_d_tiles = ff // tf, d // td
    gather_rows = rows // 16
    mm_rows = rows // 4
    assert rows % 128 == 0 and gather_rows % 8 == 0 and d % td == 0
    body = functools.partial(_experts_body, rows=rows, gather_rows=gather_rows, mm_rows=mm_rows,
                             n_ff_tiles=n_ff_tiles)
    last_ff = n_ff_tiles - 1
    down = lambda j: jnp.maximum(j - n_ff_tiles, 0)
    return pl.pallas_call(
        body,
        grid=(ne, n_ff_tiles + 2 * n_d_tiles),
        in_specs=[pl.BlockSpec((1, 1, rows), lambda e, j: (e, 0, 0)),
                  pl.BlockSpec((1, rows, 1), lambda e, j: (e, 0, 0)),
                  pl.BlockSpec(memory_space=pl.ANY),
                  pl.BlockSpec((1, d, tf), lambda e, j: (e, 0, jnp.minimum(j, last_ff))),
                  pl.BlockSpec((1, d, tf), lambda e, j: (e, 0, jnp.minimum(j, last_ff))),
                  pl.BlockSpec((1, ff, td), lambda e, j: (e, 0, down(j) // 2))],
        out_specs=pl.BlockSpec((1, rows // 2, SUBLANES, LANES), lambda e, j: (e, down(j) % 2, down(j) // 2, 0)),
        out_shape=jax.ShapeDtypeStruct((ne, rows, n_lane_blocks, LANES), F32),
        scratch_shapes=[pltpu.SMEM((1, rows), jnp.int32),
                        pltpu.VMEM((rows, d), BF),
                        pltpu.VMEM((n_ff_tiles, rows, tf), BF),
                        pltpu.VMEM((2, gather_rows, n_lane_blocks, LANES), F32),
                        pltpu.SemaphoreType.DMA((2,))],
        compiler_params=_cparams("arbitrary", "arbitrary"),
        name="experts",
    )(idx, gates, hm, wg, wu, wd)


COMBINE_TOKENS = 64


def _combine_body(src_ref, start_ref, h_ref, gate_ref, y_hbm, o_ref, buf, acc_ref, sem, *, n_tiles):
    g = pl.program_id(0)
    tt = h_ref.shape[0]
    n_lane_blocks = buf.shape[2]

    def row_copy(p, base, slot):
        return pltpu.make_async_copy(y_hbm.at[src_ref[p]], buf.at[slot, p - base], sem.at[slot])

    def issue(t, slot):
        base = start_ref[t * tt]

        def body(p, carry):
            row_copy(p, base, slot).start()
            return carry
        lax.fori_loop(base, start_ref[(t + 1) * tt], body, 0)

    def drain(t, slot):
        base = start_ref[t * tt]

        def body(p, carry):
            row_copy(p, base, slot).wait()
            return carry
        lax.fori_loop(base, start_ref[(t + 1) * tt], body, 0)

    @pl.when(g == 0)
    def _first():
        issue(0, 0)

    @pl.when(g + 1 < n_tiles)
    def _prefetch():
        issue(g + 1, (g + 1) % 2)

    slot = g % 2
    drain(g, slot)
    base = start_ref[g * tt]

    def per_token(t, carry):
        lo = start_ref[g * tt + t] - base
        hi = start_ref[g * tt + t + 1] - base
        acc_ref[t] = lax.fori_loop(lo, hi, lambda p, acc: acc + buf[slot, p],
                                   jnp.zeros((n_lane_blocks, LANES), F32))
        return carry
    lax.fori_loop(0, tt, per_token, 0)

    gate = gate_ref[0]
    for lb in range(n_lane_blocks):
        cols = slice(lb * LANES, (lb + 1) * LANES)
        o_ref[:, cols] = h_ref[:, cols] + gate[:, cols] * acc_ref[:, lb, :]


def _combine(h, y_slabs, gate, tok, yrow, row_off, n_tok, rows_per_group, max_per_token):
    tt = COMBINE_TOKENS
    d = h.shape[1]
    n_lane_blocks = y_slabs.shape[1]
    n_tiles = n_tok // tt
    assert n_tok % tt == 0 and row_off % tt == 0 and n_lane_blocks * LANES == d
    order = jnp.argsort(tok)
    tok_s = tok[order]
    start = jnp.searchsorted(tok_s, jnp.arange(n_tok + 1, dtype=jnp.int32), side='left').astype(jnp.int32)
    src = yrow[order].astype(jnp.int32)
    tile0 = row_off // tt
    per = rows_per_group // tt
    n_groups = gate.shape[0]
    grid_spec = pltpu.PrefetchScalarGridSpec(
        num_scalar_prefetch=2,
        grid=(n_tiles,),
        in_specs=[pl.BlockSpec((tt, d), lambda g, *_: (tile0 + g, 0)),
                  pl.BlockSpec((1, 1, d), lambda g, *_: (jnp.minimum(g // per, n_groups - 1), 0, 0)),
                  pl.BlockSpec(memory_space=pl.ANY)],
        out_specs=pl.BlockSpec((tt, d), lambda g, *_: (tile0 + g, 0)),
        scratch_shapes=[pltpu.VMEM((2, max_per_token * tt, n_lane_blocks, LANES), F32),
                        pltpu.VMEM((tt, n_lane_blocks, LANES), F32),
                        pltpu.SemaphoreType.DMA((2,))],
    )
    return pl.pallas_call(
        functools.partial(_combine_body, n_tiles=n_tiles),
        grid_spec=grid_spec,
        out_shape=jax.ShapeDtypeStruct(h.shape, F32),
        input_output_aliases={2: 0},
        compiler_params=_cparams("arbitrary"),
        name="combine",
    )(src, start, h, gate, y_slabs)


ATT_Q_TILE = 512


def _rope(x, cos, sin):
    lane = lax.broadcasted_iota(jnp.int32, x.shape, 1)
    fwd = pltpu.roll(x, AXIS_DIM // 2, axis=1)
    bwd = pltpu.roll(x, HEAD_DIM - AXIS_DIM // 2, axis=1)
    rot = jnp.where(lane % AXIS_DIM < AXIS_DIM // 2, -bwd, fwd)
    return x * cos + rot * sin


def _attn_body(q_ref, kp_ref, km_ref, kn_ref, vp_ref, vm_ref, vn_ref, kc_ref, vc_ref,
               cm_ref, sm_ref, cp_ref, sp_ref, cn_ref, sn_ref, sink_ref, o_ref, *, n_q_tiles):
    i = pl.program_id(2)
    qt = q_ref.shape[0]
    blk = ATT_BLOCK
    nb = qt // blk
    scale = HEAD_DIM ** -0.5
    cos_w = jnp.concatenate([cp_ref[...], cm_ref[...], cn_ref[...]], axis=0)
    sin_w = jnp.concatenate([sp_ref[...], sm_ref[...], sn_ref[...]], axis=0)
    k_w = jnp.concatenate([kp_ref[...], km_ref[...], kn_ref[...]], axis=0)
    k_w = _rope(k_w, cos_w, sin_w).astype(BF)
    v_w = jnp.concatenate([vp_ref[...], vm_ref[...], vn_ref[...]], axis=0).astype(BF)
    kc = kc_ref[...].astype(BF)
    vc = vc_ref[...].astype(BF)
    sink = sink_ref[0]
    nt = (((1,), (1,)), ((), ()))
    band = 3 * blk
    row = lax.broadcasted_iota(jnp.int32, (GQ * blk, band), 0) % blk
    col = lax.broadcasted_iota(jnp.int32, (GQ * blk, band), 1)
    in_window = (col >= row) & (col <= row + 2 * WINDOW)
    for jb in range(nb):
        rs = slice(jb * blk, (jb + 1) * blk)
        cos_q = cm_ref[rs, :]
        sin_q = sm_ref[rs, :]
        qs = jnp.concatenate([_rope(q_ref[rs, g * HEAD_DIM:(g + 1) * HEAD_DIM], cos_q, sin_q)
                              for g in range(GQ)], axis=0).astype(BF)
        kb = k_w[jb * blk:jb * blk + band]
        vb = v_w[jb * blk:jb * blk + band]
        s_loc = lax.dot_general(qs, kb, nt, preferred_element_type=F32) * scale
        s_ctx = lax.dot_general(qs, kc, nt, preferred_element_type=F32) * scale
        valid = in_window
        if jb == 0:
            valid = valid & (col >= jnp.where(i == 0, blk, 0))
        if jb == nb - 1:
            valid = valid & (col < jnp.where(i == n_q_tiles - 1, 2 * blk, band))
        s_loc = jnp.where(valid, s_loc, NEG_INF)
        m = jnp.maximum(jnp.maximum(jnp.max(s_loc, axis=1, keepdims=True),
                                    jnp.max(s_ctx, axis=1, keepdims=True)), sink)
        e_loc = jnp.exp(s_loc - m)
        e_ctx = jnp.exp(s_ctx - m)
        den = (jnp.sum(e_loc, axis=1, keepdims=True) + jnp.sum(e_ctx, axis=1, keepdims=True)
               + jnp.exp(sink - m))
        o = (jnp.dot(e_loc.astype(BF), vb, preferred_element_type=F32)
             + jnp.dot(e_ctx.astype(BF), vc, preferred_element_type=F32)) / den
        for g in range(GQ):
            o_ref[rs, g * HEAD_DIM:(g + 1) * HEAD_DIM] = o[g * blk:(g + 1) * blk].astype(o_ref.dtype)


def _attention(p, cos, sin, sinks, n_batch, seq, ctx_len):
    qt, blk = ATT_Q_TILE, ATT_BLOCK
    ml = n_batch * seq
    nq = seq // qt
    nbk = seq // blk
    bpt = qt // blk
    kcol, vcol = Q_W // HEAD_DIM, (Q_W + KV_W) // HEAD_DIM
    assert seq % qt == 0 and ml % ctx_len == 0
    sink_col = jnp.repeat(sinks.astype(F32).reshape(KV_HEADS, GQ, 1), blk, axis=2).reshape(KV_HEADS, GQ * blk, 1)
    prev_blk = lambda i: jnp.maximum(i * bpt - 1, 0)
    next_blk = lambda i: jnp.minimum((i + 1) * bpt, nbk - 1)
    qspec = pl.BlockSpec((qt, GQ * HEAD_DIM), lambda b, h, i: (b * nq + i, h))

    def kv_specs(c0):
        return [pl.BlockSpec((blk, HEAD_DIM), lambda b, h, i: (b * nbk + prev_blk(i), c0 + h)),
                pl.BlockSpec((qt, HEAD_DIM), lambda b, h, i: (b * nq + i, c0 + h)),
                pl.BlockSpec((blk, HEAD_DIM), lambda b, h, i: (b * nbk + next_blk(i), c0 + h))]

    ctx_spec = lambda c0: pl.BlockSpec((ctx_len, HEAD_DIM), lambda b, h, i: (ml // ctx_len + b, c0 + h))
    tab_specs = [pl.BlockSpec((qt, HEAD_DIM), lambda b, h, i: (i, 0)),
                 pl.BlockSpec((qt, HEAD_DIM), lambda b, h, i: (i, 0)),
                 pl.BlockSpec((blk, HEAD_DIM), lambda b, h, i: (prev_blk(i), 0)),
                 pl.BlockSpec((blk, HEAD_DIM), lambda b, h, i: (prev_blk(i), 0)),
                 pl.BlockSpec((blk, HEAD_DIM), lambda b, h, i: (next_blk(i), 0)),
                 pl.BlockSpec((blk, HEAD_DIM), lambda b, h, i: (next_blk(i), 0))]
    return pl.pallas_call(
        functools.partial(_attn_body, n_q_tiles=nq),
        grid=(n_batch, KV_HEADS, nq),
        in_specs=[qspec] + kv_specs(kcol) + kv_specs(vcol) + [ctx_spec(kcol), ctx_spec(vcol)] + tab_specs
                 + [pl.BlockSpec((1, GQ * blk, 1), lambda b, h, i: (h, 0, 0))],
        out_specs=pl.BlockSpec((qt, GQ * HEAD_DIM), lambda b, h, i: (b * nq + i, h)),
        out_shape=jax.ShapeDtypeStruct((ml, Q_W), BF),
        compiler_params=_cparams("parallel", "parallel", "arbitrary"),
        name="attention",
    )(p, p, p, p, p, p, p, p, p, cos, sin, cos, sin, cos, sin, sink_col)


GLA_TOKENS = 256
_NT = (((1,), (1,)), ((), ()))
_TN = (((0,), (0,)), ((), ()))


def _gla_body(*refs, reverse, fuse_output):
    if fuse_output:
        q_ref, k_ref, v_ref, lr_ref, gw_ref, gb_ref, of_ref, g_ref, gain_ref, o_ref, st_ref = refs
    else:
        q_ref, k_ref, v_ref, lr_ref, gw_ref, gb_ref, o_ref, st_ref = refs
    tb = q_ref.shape[0]
    ch = GLA_CHUNK
    n_ch = tb // ch

    @pl.when(pl.program_id(2) == 0)
    def _reset():
        st_ref[...] = jnp.zeros(st_ref.shape, F32)

    r = lax.broadcasted_iota(jnp.int32, (tb, tb), 0)
    c = lax.broadcasted_iota(jnp.int32, (tb, tb), 1)
    same = (r // ch) == (c // ch)
    tri = same & ((c >= r) if reverse else (c <= r))
    tri_bf = jnp.where(tri, 1.0, 0.0).astype(BF)
    same_bf = jnp.where(same, 1.0, 0.0).astype(BF)

    z = jnp.dot(lr_ref[...].astype(BF), gw_ref[...], preferred_element_type=F32) + gb_ref[...]
    la = jax.nn.log_sigmoid(z) * (1.0 / GLA_TAU)
    la_hi = la.astype(BF)
    la_lo = (la - la_hi.astype(F32)).astype(BF)
    b = (jnp.dot(tri_bf, la_hi, preferred_element_type=F32)
         + jnp.dot(tri_bf, la_lo, preferred_element_type=F32))
    tot = (jnp.dot(same_bf, la_hi, preferred_element_type=F32)
           + jnp.dot(same_bf, la_lo, preferred_element_type=F32))
    q = q_ref[...] * (GLA_DK ** -0.5)
    k = k_ref[...]
    v = v_ref[...].astype(BF)
    q_in = (q * jnp.exp(b)).astype(BF)
    k_in = (k * jnp.exp(-b)).astype(BF)
    k_dec = (k * jnp.exp(tot - b)).astype(BF)
    dec = jnp.exp(tot)
    sc = lax.dot_general(q_in, k_in, _NT, preferred_element_type=F32)
    sc = jnp.where(tri, sc, 0.0).astype(BF)
    o_intra = jnp.dot(sc, v, preferred_element_type=F32)
    outs = [None] * n_ch
    for ci in (range(n_ch - 1, -1, -1) if reverse else range(n_ch)):
        rs = slice(ci * ch, (ci + 1) * ch)
        st = st_ref[...]
        outs[ci] = o_intra[rs] + lax.dot_general(q_in[rs], st.astype(BF), _NT, preferred_element_type=F32)
        upd = lax.dot_general(v[rs], k_dec[rs], _TN, preferred_element_type=F32)
        st_ref[...] = st * dec[ci * ch:ci * ch + 1, :] + upd
    o = jnp.concatenate(outs, axis=0)
    if fuse_output:
        o = o + of_ref[...]
        y = o * lax.rsqrt(jnp.mean(o * o, axis=-1, keepdims=True) + EPS) * gain_ref[...]
        g = g_ref[...]
        o_ref[...] = (y * (g * jax.nn.sigmoid(g))).astype(o_ref.dtype)
    else:
        o_ref[...] = o


def _gla(p, lr, gate_w, gate_b, o_fwd, gain, n_batch, seq, ctx_len, *, reverse):
    tb = GLA_TOKENS
    ml = n_batch * seq
    m = p.shape[0]
    nc, nl = ctx_len // tb, seq // tb
    assert ctx_len % tb == 0 and seq % tb == 0 and ml % tb == 0
    ctx_base = ml // tb

    def row(b, s):
        sc_ = (nc - 1 - s) if reverse else s
        sl_ = (nl - 1 - (s - nc)) if reverse else (s - nc)
        return jnp.where(s < nc, ctx_base + b * nc + sc_, b * nl + sl_)

    kq, kk, kv, kg = 0, GLA_K_W // GLA_DK, 2 * GLA_K_W // GLA_DV, (2 * GLA_K_W + GLA_V_W) // GLA_DV
    in_specs = [pl.BlockSpec((tb, GLA_DK), lambda b, h, s: (row(b, s), kq + h)),
                pl.BlockSpec((tb, GLA_DK), lambda b, h, s: (row(b, s), kk + h)),
                pl.BlockSpec((tb, GLA_DV), lambda b, h, s: (row(b, s), kv + h)),
                pl.BlockSpec((tb, LANES), lambda b, h, s: (row(b, s), 0)),
                pl.BlockSpec((LANES, GLA_DK), lambda b, h, s: (0, h)),
                pl.BlockSpec((1, GLA_DK), lambda b, h, s: (0, h))]
    args = [p, p, p, lr, gate_w, gate_b]
    if reverse:
        in_specs += [pl.BlockSpec((tb, GLA_DV), lambda b, h, s: (row(b, s), h)),
                     pl.BlockSpec((tb, GLA_DV), lambda b, h, s: (row(b, s), kg + h)),
                     pl.BlockSpec((1, GLA_DV), lambda b, h, s: (0, h))]
        args += [o_fwd, p, gain]
    return pl.pallas_call(
        functools.partial(_gla_body, reverse=reverse, fuse_output=reverse),
        grid=(n_batch, GLA_HEADS, nc + nl),
        in_specs=in_specs,
        out_specs=pl.BlockSpec((tb, GLA_DV), lambda b, h, s: (row(b, s), h)),
        out_shape=jax.ShapeDtypeStruct((m, GLA_V_W), BF if reverse else F32),
        scratch_shapes=[pltpu.VMEM((GLA_DV, GLA_DK), F32)],
        compiler_params=_cparams("parallel", "parallel", "arbitrary"),
        name="gla_bwd" if reverse else "gla_fwd",
    )(*args)


SG_TOKENS = 512


def _sg_body(u_ref, vs_ref, nw_ref, nb_ref, ws_ref, bs_ref, o_ref):
    u = jax.nn.gelu(u_ref[...])
    vf = jax.nn.gelu(vs_ref[...])
    mu = jnp.mean(vf, axis=-1, keepdims=True)
    var = jnp.mean(jnp.square(vf - mu), axis=-1, keepdims=True)
    vn = ((vf - mu) * lax.rsqrt(var + EPS) * nw_ref[...] + nb_ref[...]).astype(BF)
    ws = ws_ref[0].astype(BF)
    bias = bs_ref[0]
    for ci in range(u.shape[0] // SG_CHUNK):
        rs = slice(ci * SG_CHUNK, (ci + 1) * SG_CHUNK)
        s = jnp.dot(ws, vn[rs], preferred_element_type=F32) + bias
        o_ref[rs, :] = (u[rs] * s).astype(o_ref.dtype)


def _spatial_gate(p, nw, nb, ws, bias_col):
    m = p.shape[0]
    tt = SG_TOKENS
    ku = (2 * GLA_K_W + 2 * GLA_V_W) // SG_GC
    kv = ku + SG_W // SG_GC
    return pl.pallas_call(
        _sg_body,
        grid=(m // tt, SG_GROUPS),
        in_specs=[pl.BlockSpec((tt, SG_GC), lambda i, g: (i, ku + g)),
                  pl.BlockSpec((tt, SG_GC), lambda i, g: (i, kv + g)),
                  pl.BlockSpec((1, SG_GC), lambda i, g: (0, g)),
                  pl.BlockSpec((1, SG_GC), lambda i, g: (0, g)),
                  pl.BlockSpec((1, SG_CHUNK, SG_CHUNK), lambda i, g: (g, 0, 0)),
                  pl.BlockSpec((1, SG_CHUNK, 1), lambda i, g: (g, 0, 0))],
        out_specs=pl.BlockSpec((tt, SG_GC), lambda i, g: (i, g)),
        out_shape=jax.ShapeDtypeStruct((m, SG_W), BF),
        compiler_params=_cparams("parallel", "arbitrary"),
        name="spatial_gate",
    )(p, p, nw, nb, ws, bias_col)


def _mm_res2_body(a_ref, b_ref, wa_ref, wb_ref, r_ref, g_ref, o_ref):
    acc = (jnp.dot(a_ref[...], wa_ref[...], preferred_element_type=F32)
           + jnp.dot(b_ref[...], wb_ref[...], preferred_element_type=F32))
    o_ref[...] = r_ref[...] + g_ref[0] * acc


def _mm_res2(a, b, w, res, gate, rows_per_group, tm, tn):
    m, ka = a.shape
    kb = b.shape[1]
    n = w.shape[1]
    g = gate.shape[0]
    per = rows_per_group // tm
    assert ka == kb and w.shape[0] == ka + kb and m % tm == 0 and n % tn == 0 and rows_per_group % tm == 0
    return pl.pallas_call(
        _mm_res2_body,
        grid=(m // tm, n // tn),
        in_specs=[pl.BlockSpec((tm, ka), lambda i, j: (i, 0)),
                  pl.BlockSpec((tm, kb), lambda i, j: (i, 0)),
                  pl.BlockSpec((ka, tn), lambda i, j: (0, j)),
                  pl.BlockSpec((kb, tn), lambda i, j: (1, j)),
                  pl.BlockSpec((tm, tn), lambda i, j: (i, j)),
                  pl.BlockSpec((1, 1, tn), lambda i, j: (jnp.minimum(i // per, g - 1), 0, j))],
        out_specs=pl.BlockSpec((tm, tn), lambda i, j: (i, j)),
        out_shape=jax.ShapeDtypeStruct((m, n), F32),
        compiler_params=_cparams("parallel", "arbitrary"),
        name="mm_res2",
    )(a, b, w, w, res, gate)


def _axial_rope_tables(T):
    rows = T // GRID_W
    row, col = jnp.meshgrid(jnp.arange(rows), jnp.arange(GRID_W), indexing='ij')
    inv_freq = ROPE_BASE ** (-jnp.arange(0, AXIS_DIM, 2, dtype=jnp.float32) / AXIS_DIM)

    def angles(pos):
        ang = pos.reshape(-1).astype(jnp.float32)[:, None] * inv_freq
        return jnp.concatenate([ang, ang], axis=-1)

    ang = jnp.concatenate([angles(row), angles(col)], axis=-1)
    return jnp.cos(ang), jnp.sin(ang)


def _moe(hall, groups, gain, shift, scale, gate, rw_t, wg, wu, wd):
    m, d = hall.shape
    ne = rw_t.shape[0]
    set_len0 = groups[0][2]
    hm, aff = _router(hall, gain, shift, scale, rw_t, set_len0, 512)
    idx_parts, gate_parts, meta = [], [], []
    r_off = 0
    for row_off, n_sets, set_len in groups:
        cap = EC_CAPACITY * set_len // ne
        a = aff[:, row_off:row_off + n_sets * set_len].reshape(ne, n_sets, set_len)
        gts, idx = lax.top_k(a, cap)
        rows = idx + (row_off + jnp.arange(n_sets, dtype=jnp.int32) * set_len)[None, :, None]
        idx_parts.append(rows.reshape(ne, n_sets * cap))
        gate_parts.append(gts.reshape(ne, n_sets * cap))
        meta.append((row_off, n_sets, set_len, cap, r_off, idx))
        r_off += n_sets * cap
    rows_per_expert = r_off
    idx_all = jnp.concatenate(idx_parts, axis=1).astype(jnp.int32)
    gates_all = jnp.concatenate(gate_parts, axis=1)
    y = _experts(hm, idx_all.reshape(ne, 1, rows_per_expert), gates_all.reshape(ne, rows_per_expert, 1),
                 wg, wu, wd)
    y2d = y.reshape(ne * rows_per_expert, d // LANES, LANES)
    e_base = (jnp.arange(ne, dtype=jnp.int32) * rows_per_expert)[:, None]
    for gi, (row_off, n_sets, set_len, cap, r0, idx) in enumerate(meta):
        if gi == 0:
            for s in range(n_sets):
                tok = idx[:, s, :].reshape(-1).astype(jnp.int32)
                yrow = (e_base + r0 + s * cap + jnp.arange(cap, dtype=jnp.int32)[None, :]).reshape(-1)
                hall = _combine(hall, y2d, gate[s:s + 1], tok, yrow, row_off + s * set_len, set_len, set_len, ne)
        else:
            tok = (idx + (jnp.arange(n_sets, dtype=jnp.int32) * set_len)[None, :, None]).reshape(-1).astype(jnp.int32)
            yrow = (e_base + r0 + jnp.arange(n_sets * cap, dtype=jnp.int32)[None, :]).reshape(-1)
            hall = _combine(hall, y2d, gate[-1:], tok, yrow, row_off, n_sets * set_len, n_sets * set_len, ne)
    return hall


def kernel(x, c, ctx, c_ctx, mod_w, mod_b, norm_mix, norm_ffn, ab_w_in, gla_gate_w, gla_gate_b, gla_norm,
           sg_norm_w, sg_norm_b, sg_ws, sg_bs, ab_w_out, c_w_in, sinks, c_w_out, router_w, moe_w_gate,
           moe_w_up, moe_w_down, final_norm):
    B, T, D = x.shape
    L = ctx.shape[1]
    ML, MC = B * T, B * L
    cos, sin = _axial_rope_tables(T)

    cond = jnp.concatenate([jax.nn.silu(c), jax.nn.silu(c_ctx)[None], jnp.zeros((3, D), x.dtype)], axis=0)
    hall = jnp.concatenate([x.reshape(ML, D), ctx.reshape(MC, D)], axis=0)

    for layer in range(DEPTH):
        last = layer == DEPTH - 1
        mods = _mm(cond.astype(BF), mod_w[layer].astype(BF), 8, 2048)[:B + 1] + mod_b[layer]
        mods = mods.reshape(B + 1, N_MOD, 1, D)
        mod = lambda i: mods[:, i]

        xall = _norm_mod(hall, norm_mix[layer], mod(0), mod(1), T, 512, BF)

        if layer % 2 == 0:
            e = layer // 2
            w_in = ab_w_in[e]
            w_main = jnp.concatenate([w_in[:, :6144], w_in[:, 6176:]], axis=1).astype(BF)
            w_lr = jnp.pad(w_in[:, 6144:6176], ((0, 0), (0, 96))).astype(BF)
            p = _mm(xall, w_main, 1024, 1024)
            lr = _mm(xall, w_lr, 1024, LANES)
            gw = [jnp.zeros((LANES, GLA_K_W), F32).at[d * GLA_RANK:(d + 1) * GLA_RANK].set(gla_gate_w[e, d]).astype(BF)
                  for d in range(2)]
            o_fwd = _gla(p, lr, gw[0], gla_gate_b[e, 0:1], None, None, B, T, L, reverse=False)
            a_mix = _gla(p, lr, gw[1], gla_gate_b[e, 1:2], o_fwd, gla_norm[e].reshape(1, GLA_V_W), B, T, L,
                         reverse=True)
            b_mix = _spatial_gate(p, sg_norm_w[e].reshape(1, SG_W), sg_norm_b[e].reshape(1, SG_W), sg_ws[e],
                                  sg_bs[e].reshape(SG_GROUPS, SG_CHUNK, 1))
            hall = _mm_res2(a_mix, b_mix, ab_w_out[e].astype(BF), hall, mod(2), T, 1024, 1024)
        else:
            o = layer // 2
            p = _mm(xall, c_w_in[o].astype(BF), 1024, 1024)
            att = _attention(p, cos, sin, sinks[o], B, T, L)
            hall = _mm_res(att, c_w_out[o].astype(BF), hall[:ML], mods[:B, 2], T, 1024, 1024)

        rw_t = router_w[layer].T.astype(BF)
        wg, wu, wd = moe_w_gate[layer].astype(BF), moe_w_up[layer].astype(BF), moe_w_down[layer].astype(BF)
        if last:
            hall = _moe(hall, [(0, B, T)], norm_ffn[layer], mods[:B, 3], mods[:B, 4], mods[:B, 5], rw_t, wg, wu, wd)
        else:
            hall = _moe(hall, [(0, B, T), (ML, B, L)], norm_ffn[layer], mod(3), mod(4), mod(5), rw_t, wg, wu, wd)

    zero = jnp.zeros((1, 1, D), jnp.float32)
    out = _norm_mod(hall[:ML], final_norm, zero, zero, ML, 512, jnp.float32)
    return out.reshape(B, T, D)
```

```python
import functools

import jax
import jax.numpy as jnp
import numpy as np
from jax import lax
from jax.experimental import pallas as pl
from jax.experimental.pallas import tpu as pltpu

D_MODEL = 4096
DEPTH = 2
GRID_W = 64
EPS = 1e-6
NEG_INF = -1e30
N_MOD = 6
MIX_W = D_MODEL
GLA_HEADS = 4
GLA_V_W = MIX_W // 2
GLA_DV = GLA_V_W // GLA_HEADS
GLA_K_W = GLA_V_W // 2
GLA_DK = GLA_K_W // GLA_HEADS
GLA_RANK = 16
GLA_TAU = 16.0
GLA_CHUNK = 64
SG_GROUPS = 4
SG_W = MIX_W // 2
SG_GC = SG_W // SG_GROUPS
SG_CHUNK = 128
ATT_HEADS = D_MODEL // 128
KV_HEADS = 8
HEAD_DIM = 128
GQ = ATT_HEADS // KV_HEADS
WINDOW = 128
ATT_BLOCK = 128
AXIS_DIM = HEAD_DIM // 2
ROPE_BASE = 10000.0
Q_W = ATT_HEADS * HEAD_DIM
KV_W = KV_HEADS * HEAD_DIM
N_EXPERTS = 16
EC_CAPACITY = 2

V7X_VMEM_LIMIT_BYTES = 56 * 1024 * 1024
LANES = 128

BF = jnp.bfloat16
F32 = jnp.float32


def _cparams(*sem):
    return pltpu.CompilerParams(dimension_semantics=sem, vmem_limit_bytes=V7X_VMEM_LIMIT_BYTES)


def _mm_body(x_ref, w_ref, o_ref):
    o_ref[...] = jnp.dot(x_ref[...], w_ref[...], preferred_element_type=F32).astype(o_ref.dtype)


def _mm(x, w, tm, tn, out_dtype=F32):
    m, k = x.shape
    n = w.shape[1]
    assert m % tm == 0 and n % tn == 0, (m, n, tm, tn)
    return pl.pallas_call(
        _mm_body,
        grid=(m // tm, n // tn),
        in_specs=[pl.BlockSpec((tm, k), lambda i, j: (i, 0)),
                  pl.BlockSpec((k, tn), lambda i, j: (0, j))],
        out_specs=pl.BlockSpec((tm, tn), lambda i, j: (i, j)),
        out_shape=jax.ShapeDtypeStruct((m, n), out_dtype),
        compiler_params=_cparams("parallel", "arbitrary"),
        name="mm",
    )(x, w)


def _mm_bias_body(x_ref, w_ref, b_ref, o_ref):
    o_ref[...] = jnp.dot(x_ref[...], w_ref[...].astype(BF), preferred_element_type=F32) + b_ref[...]


def _mm_bias(x, w, bias, tn):
    m, k = x.shape
    n = w.shape[1]
    assert n % tn == 0
    return pl.pallas_call(
        _mm_bias_body,
        grid=(n // tn,),
        in_specs=[pl.BlockSpec((m, k), lambda j: (0, 0)),
                  pl.BlockSpec((k, tn), lambda j: (0, j)),
                  pl.BlockSpec((1, tn), lambda j: (0, j))],
        out_specs=pl.BlockSpec((m, tn), lambda j: (0, j)),
        out_shape=jax.ShapeDtypeStruct((m, n), F32),
        compiler_params=_cparams("parallel"),
        name="mm_bias",
    )(x, w, bias)


def _mm_res_body(x_ref, w_ref, r_ref, g_ref, o_ref):
    acc = jnp.dot(x_ref[...], w_ref[...], preferred_element_type=F32)
    o_ref[...] = r_ref[...] + g_ref[0] * acc


def _mm_res(x, w, res, gate, rows_per_group, tm, tn):
    m, k = x.shape
    n = w.shape[1]
    g = gate.shape[0]
    per = rows_per_group // tm
    assert m % tm == 0 and n % tn == 0 and rows_per_group % tm == 0
    return pl.pallas_call(
        _mm_res_body,
        grid=(m // tm, n // tn),
        in_specs=[pl.BlockSpec((tm, k), lambda i, j: (i, 0)),
                  pl.BlockSpec((k, tn), lambda i, j: (0, j)),
                  pl.BlockSpec((tm, tn), lambda i, j: (i, j)),
                  pl.BlockSpec((1, 1, tn), lambda i, j: (jnp.minimum(i // per, g - 1), 0, j))],
        out_specs=pl.BlockSpec((tm, tn), lambda i, j: (i, j)),
        out_shape=jax.ShapeDtypeStruct((m, n), F32),
        compiler_params=_cparams("parallel", "arbitrary"),
        name="mm_res",
    )(x, w, res, gate)


def _modulated(x, gain, shift, scale):
    y = x * lax.rsqrt(jnp.mean(x * x, axis=-1, keepdims=True) + EPS)
    return (y * gain) * (1.0 + scale) + shift


def _norm_mod_body(x_ref, g_ref, sh_ref, sc_ref, o_ref):
    o_ref[...] = _modulated(x_ref[...], g_ref[...], sh_ref[0], sc_ref[0]).astype(o_ref.dtype)


def _group_index(rows_per_group, tm, groups):
    per = rows_per_group // tm
    assert rows_per_group % tm == 0
    return lambda i: (jnp.minimum(i // per, groups - 1), 0, 0)


def _norm_mod(h, gain, shift, scale, rows_per_group, tm, out_dtype):
    m, d = h.shape
    grp = _group_index(rows_per_group, tm, shift.shape[0])
    return pl.pallas_call(
        _norm_mod_body,
        grid=(m // tm,),
        in_specs=[pl.BlockSpec((tm, d), lambda i: (i, 0)),
                  pl.BlockSpec((1, d), lambda i: (0, 0)),
                  pl.BlockSpec((1, 1, d), grp),
                  pl.BlockSpec((1, 1, d), grp)],
        out_specs=pl.BlockSpec((tm, d), lambda i: (i, 0)),
        out_shape=jax.ShapeDtypeStruct((m, d), out_dtype),
        compiler_params=_cparams("parallel"),
        name="norm_mod",
    )(h, gain.reshape(1, d), shift, scale)


def _router_body(x_ref, g_ref, sh_ref, sc_ref, rwt_ref, hm_ref, aff_ref):
    hm = _modulated(x_ref[...], g_ref[...], sh_ref[0], sc_ref[0])
    hm_ref[...] = hm
    logits = lax.dot_general(rwt_ref[...], hm.astype(BF), (((1,), (1,)), ((), ())),
                             preferred_element_type=F32)
    e = jnp.exp(logits - jnp.max(logits, axis=0, keepdims=True))
    aff_ref[...] = e / jnp.sum(e, axis=0, keepdims=True)


def _router(h, gain, shift, scale, rw_t, rows_per_group, tm):
    m, d = h.shape
    ne = rw_t.shape[0]
    grp = _group_index(rows_per_group, tm, shift.shape[0])
    return pl.pallas_call(
        _router_body,
        grid=(m // tm,),
        in_specs=[pl.BlockSpec((tm, d), lambda i: (i, 0)),
                  pl.BlockSpec((1, d), lambda i: (0, 0)),
                  pl.BlockSpec((1, 1, d), grp),
                  pl.BlockSpec((1, 1, d), grp),
                  pl.BlockSpec((ne, d), lambda i: (0, 0))],
        out_specs=[pl.BlockSpec((tm, d), lambda i: (i, 0)),
                   pl.BlockSpec((ne, tm), lambda i: (0, i))],
        out_shape=[jax.ShapeDtypeStruct((m, d), F32), jax.ShapeDtypeStruct((ne, m), F32)],
        compiler_params=_cparams("parallel"),
        name="router",
    )(h, gain.reshape(1, d), shift, scale, rw_t)


GATHER_ROWS = 128
GATHER_BUFS = 4


def _experts_body(idx_vm, idx_next_vm, gates_ref, hm_hbm, wg_ref, wu_ref, wd_ref, y_ref,
                  idx_sm, xg, hid, stage, sem, *, rows, mm_rows, n_ff_tiles, n_d_tiles):
    e = pl.program_id(0)
    j = pl.program_id(1)
    n_experts = pl.num_programs(0)
    ch, nb = GATHER_ROWS, GATHER_BUFS
    n_chunks = rows // ch
    tf = hid.shape[2]

    def issue(q):
        def body(r, carry):
            src = idx_sm[0, q * ch + r]
            pltpu.make_async_copy(hm_hbm.at[pl.ds(src, 1)], stage.at[q % nb, pl.ds(r, 1)], sem.at[q % nb]).start()
            return carry
        lax.fori_loop(0, ch, body, 0)

    def finish(q):
        pltpu.make_async_copy(hm_hbm.at[pl.ds(0, ch)], stage.at[q % nb], sem.at[q % nb]).wait()
        xg[q * ch:(q + 1) * ch, :] = stage[q % nb].astype(BF)

    def gather(lo, hi, first):
        if first:
            for q in range(min(nb, n_chunks)):
                issue(q)
        for q in range(lo, hi):
            finish(q)
            if q + nb < n_chunks:
                issue(q + nb)

    @pl.when((e == 0) & (j == 0))
    def _first_gather():
        pltpu.sync_copy(idx_vm.at[0], idx_sm)
        gather(0, n_chunks, True)

    for k in range(n_d_tiles):
        @pl.when((j == n_ff_tiles + k) & (e + 1 < n_experts))
        def _next_gather(k=k):
            if k == 0:
                pltpu.sync_copy(idx_next_vm.at[0], idx_sm)
            gather(k * n_chunks // n_d_tiles, (k + 1) * n_chunks // n_d_tiles, k == 0)

    @pl.when(j < n_ff_tiles)
    def _up():
        for rc in range(rows // mm_rows):
            rs = slice(rc * mm_rows, (rc + 1) * mm_rows)
            xs = xg[rs, :]
            g = jnp.dot(xs, wg_ref[0], preferred_element_type=F32)
            u = jnp.dot(xs, wu_ref[0], preferred_element_type=F32)
            hid[j, rs, :] = ((g * jax.nn.sigmoid(g)) * u).astype(BF)

    @pl.when(j >= n_ff_tiles)
    def _down():
        for rc in range(rows // mm_rows):
            rs = slice(rc * mm_rows, (rc + 1) * mm_rows)
            acc = jnp.dot(hid[0, rs, :], wd_ref[0, 0:tf, :], preferred_element_type=F32)
            for f in range(1, n_ff_tiles):
                acc = acc + jnp.dot(hid[f, rs, :], wd_ref[0, f * tf:(f + 1) * tf, :], preferred_element_type=F32)
            y_ref[0, rs, :] = acc * gates_ref[0, rs, :]


def _experts(hm, idx, gates, wg, wu, wd, tf=256, td=512):
    ne, _, rows = idx.shape
    d = hm.shape[1]
    ff = wg.shape[2]
    n_ff_tiles, n_d_tiles = ff // tf, d // td
    mm_rows = rows // 4
    assert rows % GATHER_ROWS == 0 and mm_rows % 8 == 0 and n_d_tiles <= rows // GATHER_ROWS
    body = functools.partial(_experts_body, rows=rows, mm_rows=mm_rows, n_ff_tiles=n_ff_tiles, n_d_tiles=n_d_tiles)
    last_ff = n_ff_tiles - 1
    return pl.pallas_call(
        body,
        grid=(ne, n_ff_tiles + n_d_tiles),
        in_specs=[pl.BlockSpec((1, 1, rows), lambda e, j: (e, 0, 0)),
                  pl.BlockSpec((1, 1, rows), lambda e, j: (jnp.minimum(e + 1, ne - 1), 0, 0)),
                  pl.BlockSpec((1, rows, 1), lambda e, j: (e, 0, 0)),
                  pl.BlockSpec(memory_space=pl.ANY),
                  pl.BlockSpec((1, d, tf), lambda e, j: (e, 0, jnp.minimum(j, last_ff))),
                  pl.BlockSpec((1, d, tf), lambda e, j: (e, 0, jnp.minimum(j, last_ff))),
                  pl.BlockSpec((1, ff, td), lambda e, j: (e, 0, jnp.maximum(j - n_ff_tiles, 0)))],
        out_specs=pl.BlockSpec((1, rows, td), lambda e, j: (e, 0, jnp.maximum(j - n_ff_tiles, 0))),
        out_shape=jax.ShapeDtypeStruct((ne, rows, d), F32),
        scratch_shapes=[pltpu.SMEM((1, rows), jnp.int32),
                        pltpu.VMEM((rows, d), BF),
                        pltpu.VMEM((n_ff_tiles, rows, tf), BF),
                        pltpu.VMEM((GATHER_BUFS, GATHER_ROWS, d), F32),
                        pltpu.SemaphoreType.DMA((GATHER_BUFS,))],
        compiler_params=_cparams("arbitrary", "arbitrary"),
        name="experts",
    )(idx, idx, gates, hm, wg, wu, wd)


COMBINE_TOKENS = 32
COMBINE_BUFS = 4
COMBINE_STRIP = 512


def _combine_body(pk_ref, ts_ref, mc_ref, h_ref, cnt_ref, gate_ref, y_hbm, o_ref, buf, sem, *, n_tiles, n_slots):
    g = pl.program_id(0)
    tt, d = h_ref.shape
    nb = buf.shape[0]

    def issue(t):
        slot = t % nb

        def body(p, carry):
            v = pk_ref[p]
            src = jnp.bitwise_and(v, 0xFFFF)
            dst = jnp.right_shift(v, 16)
            pltpu.make_async_copy(y_hbm.at[pl.ds(src, 1)], buf.at[slot, pl.ds(dst, 1)], sem.at[slot]).start()
            return carry
        lax.fori_loop(ts_ref[t], ts_ref[t + 1], body, 0)

    def drain(t):
        slot = t % nb

        def body(p, carry):
            pltpu.make_async_copy(y_hbm.at[pl.ds(0, 1)], buf.at[slot, pl.ds(0, 1)], sem.at[slot]).wait()
            return carry
        lax.fori_loop(ts_ref[t], ts_ref[t + 1], body, 0)

    @pl.when(g == 0)
    def _first():
        def zero(r, carry):
            buf[:, pl.ds(pl.multiple_of(r * tt, tt), tt), :] = jnp.zeros((nb, tt, d), F32)
            return carry
        lax.fori_loop(0, n_slots, zero, 0)
        for t in range(min(nb - 1, n_tiles)):
            issue(t)

    @pl.when(g + nb - 1 < n_tiles)
    def _prefetch():
        issue(g + nb - 1)

    slot = g % nb
    drain(g)
    n_used = mc_ref[g]
    cnt = cnt_ref[...]
    for cs in range(d // COMBINE_STRIP):
        cols = slice(cs * COMBINE_STRIP, (cs + 1) * COMBINE_STRIP)

        def body(k, acc):
            v = buf[slot, pl.ds(pl.multiple_of(k * tt, tt), tt), cols]
            return acc + jnp.where(cnt > k, v, 0.0)

        acc = lax.fori_loop(0, n_used, body, jnp.zeros((tt, COMBINE_STRIP), F32))
        o_ref[:, cols] = h_ref[:, cols] + gate_ref[0][:, cols] * acc


def _combine_tables(tok, yrow, n_tok):
    tt = COMBINE_TOKENS
    s, n_pairs = tok.shape
    key = jnp.sort(jnp.bitwise_or(jnp.left_shift(tok, 16), yrow), axis=-1)
    tok_s = jnp.right_shift(key, 16)
    src = jnp.bitwise_and(key, 0xFFFF)
    cnt = jnp.sum(tok[:, :, None] == jnp.arange(n_tok, dtype=jnp.int32)[None, None, :], axis=1, dtype=jnp.int32)
    start = jnp.concatenate([jnp.zeros((s, 1), jnp.int32), jnp.cumsum(cnt, axis=1, dtype=jnp.int32)], axis=1)
    pos = jnp.broadcast_to(jnp.arange(n_pairs, dtype=jnp.int32), (s, n_pairs))
    is_first = jnp.concatenate([jnp.ones((s, 1), bool), tok_s[:, 1:] != tok_s[:, :-1]], axis=1)
    k = pos - lax.cummax(jnp.where(is_first, pos, 0), axis=1)
    packed = jnp.bitwise_or(src, jnp.left_shift(k * tt + tok_s % tt, 16))
    return packed, start[:, ::tt], jnp.max(cnt.reshape(s, n_tok // tt, tt), axis=2), cnt


def _combine(h, y2d, gate, packed, tile_start, tile_slots, cnt, row_off, rows_per_group, n_slots):
    tt = COMBINE_TOKENS
    d = h.shape[1]
    n_tok = cnt.shape[0]
    n_tiles = n_tok // tt
    assert n_tok % tt == 0 and row_off % tt == 0 and y2d.shape[0] <= 0x10000 and n_slots * tt <= 0x7FFF
    tile0 = row_off // tt
    per = rows_per_group // tt
    n_groups = gate.shape[0]
    grid_spec = pltpu.PrefetchScalarGridSpec(
        num_scalar_prefetch=3,
        grid=(n_tiles,),
        in_specs=[pl.BlockSpec((tt, d), lambda g, *_: (tile0 + g, 0)),
                  pl.BlockSpec((tt, 1), lambda g, *_: (g, 0)),
                  pl.BlockSpec((1, 1, d), lambda g, *_: (jnp.minimum(g // per, n_groups - 1), 0, 0)),
                  pl.BlockSpec(memory_space=pl.ANY)],
        out_specs=pl.BlockSpec((tt, d), lambda g, *_: (tile0 + g, 0)),
        scratch_shapes=[pltpu.VMEM((COMBINE_BUFS, n_slots * tt, d), F32),
                        pltpu.SemaphoreType.DMA((COMBINE_BUFS,))],
    )
    return pl.pallas_call(
        functools.partial(_combine_body, n_tiles=n_tiles, n_slots=n_slots),
        grid_spec=grid_spec,
        out_shape=jax.ShapeDtypeStruct(h.shape, F32),
        input_output_aliases={3: 0},
        compiler_params=_cparams("arbitrary"),
        name="combine",
    )(packed, tile_start, tile_slots, h, cnt.reshape(n_tok, 1), gate, y2d)


ATT_Q_TILE = 512


def _rope(x, cos, sin):
    lane = lax.broadcasted_iota(jnp.int32, x.shape, 1)
    fwd = pltpu.roll(x, AXIS_DIM // 2, axis=1)
    bwd = pltpu.roll(x, HEAD_DIM - AXIS_DIM // 2, axis=1)
    rot = jnp.where(lane % AXIS_DIM < AXIS_DIM // 2, -bwd, fwd)
    return x * cos + rot * sin


def _attn_body(q_ref, kp_ref, km_ref, kn_ref, vp_ref, vm_ref, vn_ref, kc_ref, vc_ref,
               cm_ref, sm_ref, cp_ref, sp_ref, cn_ref, sn_ref, sink_ref, o_ref, *, n_q_tiles):
    i = pl.program_id(2)
    qt = q_ref.shape[0]
    blk = ATT_BLOCK
    nb = qt // blk
    scale = HEAD_DIM ** -0.5
    cos_w = jnp.concatenate([cp_ref[...], cm_ref[...], cn_ref[...]], axis=0)
    sin_w = jnp.concatenate([sp_ref[...], sm_ref[...], sn_ref[...]], axis=0)
    k_w = jnp.concatenate([kp_ref[...], km_ref[...], kn_ref[...]], axis=0)
    k_w = _rope(k_w, cos_w, sin_w).astype(BF)
    v_w = jnp.concatenate([vp_ref[...], vm_ref[...], vn_ref[...]], axis=0).astype(BF)
    kc = kc_ref[...].astype(BF)
    vc = vc_ref[...].astype(BF)
    sink = sink_ref[0]
    nt = (((1,), (1,)), ((), ()))
    band = 3 * blk
    row = lax.broadcasted_iota(jnp.int32, (GQ * blk, band), 0) % blk
    col = lax.broadcasted_iota(jnp.int32, (GQ * blk, band), 1)
    in_window = (col >= row) & (col <= row + 2 * WINDOW)
    for jb in range(nb):
        rs = slice(jb * blk, (jb + 1) * blk)
        cos_q = cm_ref[rs, :]
        sin_q = sm_ref[rs, :]
        qs = jnp.concatenate([_rope(q_ref[rs, g * HEAD_DIM:(g + 1) * HEAD_DIM], cos_q, sin_q)
                              for g in range(GQ)], axis=0).astype(BF)
        kb = k_w[jb * blk:jb * blk + band]
        vb = v_w[jb * blk:jb * blk + band]
        s_loc = lax.dot_general(qs, kb, nt, preferred_element_type=F32) * scale
        s_ctx = lax.dot_general(qs, kc, nt, preferred_element_type=F32) * scale
        valid = in_window
        if jb == 0:
            valid = valid & (col >= jnp.where(i == 0, blk, 0))
        if jb == nb - 1:
            valid = valid & (col < jnp.where(i == n_q_tiles - 1, 2 * blk, band))
        s_loc = jnp.where(valid, s_loc, NEG_INF)
        m = jnp.maximum(jnp.maximum(jnp.max(s_loc, axis=1, keepdims=True),
                                    jnp.max(s_ctx, axis=1, keepdims=True)), sink)
        e_loc = jnp.exp(s_loc - m)
        e_ctx = jnp.exp(s_ctx - m)
        den = (jnp.sum(e_loc, axis=1, keepdims=True) + jnp.sum(e_ctx, axis=1, keepdims=True)
               + jnp.exp(sink - m))
        o = (jnp.dot(e_loc.astype(BF), vb, preferred_element_type=F32)
             + jnp.dot(e_ctx.astype(BF), vc, preferred_element_type=F32)) / den
        for g in range(GQ):
            o_ref[rs, g * HEAD_DIM:(g + 1) * HEAD_DIM] = o[g * blk:(g + 1) * blk].astype(o_ref.dtype)


def _attention(p, cos, sin, sinks, n_batch, seq, ctx_len):
    qt, blk = ATT_Q_TILE, ATT_BLOCK
    ml = n_batch * seq
    nq = seq // qt
    nbk = seq // blk
    bpt = qt // blk
    kcol, vcol = Q_W // HEAD_DIM, (Q_W + KV_W) // HEAD_DIM
    assert seq % qt == 0 and ml % ctx_len == 0
    sink_col = jnp.repeat(sinks.astype(F32).reshape(KV_HEADS, GQ, 1), blk, axis=2).reshape(KV_HEADS, GQ * blk, 1)
    prev_blk = lambda i: jnp.maximum(i * bpt - 1, 0)
    next_blk = lambda i: jnp.minimum((i + 1) * bpt, nbk - 1)
    qspec = pl.BlockSpec((qt, GQ * HEAD_DIM), lambda b, h, i: (b * nq + i, h))

    def kv_specs(c0):
        return [pl.BlockSpec((blk, HEAD_DIM), lambda b, h, i: (b * nbk + prev_blk(i), c0 + h)),
                pl.BlockSpec((qt, HEAD_DIM), lambda b, h, i: (b * nq + i, c0 + h)),
                pl.BlockSpec((blk, HEAD_DIM), lambda b, h, i: (b * nbk + next_blk(i), c0 + h))]

    ctx_spec = lambda c0: pl.BlockSpec((ctx_len, HEAD_DIM), lambda b, h, i: (ml // ctx_len + b, c0 + h))
    tab_specs = [pl.BlockSpec((qt, HEAD_DIM), lambda b, h, i: (i, 0)),
                 pl.BlockSpec((qt, HEAD_DIM), lambda b, h, i: (i, 0)),
                 pl.BlockSpec((blk, HEAD_DIM), lambda b, h, i: (prev_blk(i), 0)),
                 pl.BlockSpec((blk, HEAD_DIM), lambda b, h, i: (prev_blk(i), 0)),
                 pl.BlockSpec((blk, HEAD_DIM), lambda b, h, i: (next_blk(i), 0)),
                 pl.BlockSpec((blk, HEAD_DIM), lambda b, h, i: (next_blk(i), 0))]
    return pl.pallas_call(
        functools.partial(_attn_body, n_q_tiles=nq),
        grid=(n_batch, KV_HEADS, nq),
        in_specs=[qspec] + kv_specs(kcol) + kv_specs(vcol) + [ctx_spec(kcol), ctx_spec(vcol)] + tab_specs
                 + [pl.BlockSpec((1, GQ * blk, 1), lambda b, h, i: (h, 0, 0))],
        out_specs=pl.BlockSpec((qt, GQ * HEAD_DIM), lambda b, h, i: (b * nq + i, h)),
        out_shape=jax.ShapeDtypeStruct((ml, Q_W), BF),
        compiler_params=_cparams("parallel", "parallel", "arbitrary"),
        name="attention",
    )(p, p, p, p, p, p, p, p, p, cos, sin, cos, sin, cos, sin, sink_col)


GLA_TOKENS = 256
_NT = (((1,), (1,)), ((), ()))
_TN = (((0,), (0,)), ((), ()))


def _gla_body(*refs, reverse, fuse_output):
    if fuse_output:
        q_ref, k_ref, v_ref, lr_ref, gw_ref, gb_ref, of_ref, g_ref, gain_ref, o_ref, st_ref = refs
    else:
        q_ref, k_ref, v_ref, lr_ref, gw_ref, gb_ref, o_ref, st_ref = refs
    tb = q_ref.shape[0]
    ch = GLA_CHUNK
    n_ch = tb // ch

    @pl.when(pl.program_id(2) == 0)
    def _reset():
        st_ref[...] = jnp.zeros(st_ref.shape, F32)

    r = lax.broadcasted_iota(jnp.int32, (tb, tb), 0)
    c = lax.broadcasted_iota(jnp.int32, (tb, tb), 1)
    same = (r // ch) == (c // ch)
    tri = same & ((c >= r) if reverse else (c <= r))
    tri_bf = jnp.where(tri, 1.0, 0.0).astype(BF)
    same_bf = jnp.where(same, 1.0, 0.0).astype(BF)

    z = jnp.dot(lr_ref[...].astype(BF), gw_ref[...], preferred_element_type=F32) + gb_ref[...]
    la = jax.nn.log_sigmoid(z) * (1.0 / GLA_TAU)
    la_hi = la.astype(BF)
    la_lo = (la - la_hi.astype(F32)).astype(BF)
    b = (jnp.dot(tri_bf, la_hi, preferred_element_type=F32)
         + jnp.dot(tri_bf, la_lo, preferred_element_type=F32))
    tot = (jnp.dot(same_bf, la_hi, preferred_element_type=F32)
           + jnp.dot(same_bf, la_lo, preferred_element_type=F32))
    q = q_ref[...] * (GLA_DK ** -0.5)
    k = k_ref[...]
    v = v_ref[...].astype(BF)
    q_in = (q * jnp.exp(b)).astype(BF)
    k_in = (k * jnp.exp(-b)).astype(BF)
    k_dec = (k * jnp.exp(tot - b)).astype(BF)
    dec = jnp.exp(tot)
    sc = lax.dot_general(q_in, k_in, _NT, preferred_element_type=F32)
    sc = jnp.where(tri, sc, 0.0).astype(BF)
    o_intra = jnp.dot(sc, v, preferred_element_type=F32)
    outs = [None] * n_ch
    for ci in (range(n_ch - 1, -1, -1) if reverse else range(n_ch)):
        rs = slice(ci * ch, (ci + 1) * ch)
        st = st_ref[...]
        outs[ci] = o_intra[rs] + lax.dot_general(q_in[rs], st.astype(BF), _NT, preferred_element_type=F32)
        upd = lax.dot_general(v[rs], k_dec[rs], _TN, preferred_element_type=F32)
        st_ref[...] = st * dec[ci * ch:ci * ch + 1, :] + upd
    o = jnp.concatenate(outs, axis=0)
    if fuse_output:
        o = o + of_ref[...]
        y = o * lax.rsqrt(jnp.mean(o * o, axis=-1, keepdims=True) + EPS) * gain_ref[...]
        g = g_ref[...]
        o_ref[...] = (y * (g * jax.nn.sigmoid(g))).astype(o_ref.dtype)
    else:
        o_ref[...] = o


def _gla(p, lr, gate_w, gate_b, o_fwd, gain, n_batch, seq, ctx_len, *, reverse):
    tb = GLA_TOKENS
    ml = n_batch * seq
    m = p.shape[0]
    nc, nl = ctx_len // tb, seq // tb
    assert ctx_len % tb == 0 and seq % tb == 0 and ml % tb == 0
    ctx_base = ml // tb

    def row(b, s):
        sc_ = (nc - 1 - s) if reverse else s
        sl_ = (nl - 1 - (s - nc)) if reverse else (s - nc)
        return jnp.where(s < nc, ctx_base + b * nc + sc_, b * nl + sl_)

    kq, kk, kv, kg = 0, GLA_K_W // GLA_DK, 2 * GLA_K_W // GLA_DV, (2 * GLA_K_W + GLA_V_W) // GLA_DV
    in_specs = [pl.BlockSpec((tb, GLA_DK), lambda b, h, s: (row(b, s), kq + h)),
                pl.BlockSpec((tb, GLA_DK), lambda b, h, s: (row(b, s), kk + h)),
                pl.BlockSpec((tb, GLA_DV), lambda b, h, s: (row(b, s), kv + h)),
                pl.BlockSpec((tb, LANES), lambda b, h, s: (row(b, s), 0)),
                pl.BlockSpec((LANES, GLA_DK), lambda b, h, s: (0, h)),
                pl.BlockSpec((1, GLA_DK), lambda b, h, s: (0, h))]
    args = [p, p, p, lr, gate_w, gate_b]
    if reverse:
        in_specs += [pl.BlockSpec((tb, GLA_DV), lambda b, h, s: (row(b, s), h)),
                     pl.BlockSpec((tb, GLA_DV), lambda b, h, s: (row(b, s), kg + h)),
                     pl.BlockSpec((1, GLA_DV), lambda b, h, s: (0, h))]
        args += [o_fwd, p, gain]
    return pl.pallas_call(
        functools.partial(_gla_body, reverse=reverse, fuse_output=reverse),
        grid=(n_batch, GLA_HEADS, nc + nl),
        in_specs=in_specs,
        out_specs=pl.BlockSpec((tb, GLA_DV), lambda b, h, s: (row(b, s), h)),
        out_shape=jax.ShapeDtypeStruct((m, GLA_V_W), BF if reverse else F32),
        scratch_shapes=[pltpu.VMEM((GLA_DV, GLA_DK), F32)],
        compiler_params=_cparams("parallel", "parallel", "arbitrary"),
        name="gla_bwd" if reverse else "gla_fwd",
    )(*args)


SG_TOKENS = 512


def _sg_body(u_ref, vs_ref, nw_ref, nb_ref, ws_ref, bs_ref, o_ref):
    u = jax.nn.gelu(u_ref[...])
    vf = jax.nn.gelu(vs_ref[...])
    mu = jnp.mean(vf, axis=-1, keepdims=True)
    var = jnp.mean(jnp.square(vf - mu), axis=-1, keepdims=True)
    vn = ((vf - mu) * lax.rsqrt(var + EPS) * nw_ref[...] + nb_ref[...]).astype(BF)
    ws = ws_ref[0].astype(BF)
    bias = bs_ref[0]
    for ci in range(u.shape[0] // SG_CHUNK):
        rs = slice(ci * SG_CHUNK, (ci + 1) * SG_CHUNK)
        s = jnp.dot(ws, vn[rs], preferred_element_type=F32) + bias
        o_ref[rs, :] = (u[rs] * s).astype(o_ref.dtype)


def _spatial_gate(p, nw, nb, ws, bias_col):
    m = p.shape[0]
    tt = SG_TOKENS
    ku = (2 * GLA_K_W + 2 * GLA_V_W) // SG_GC
    kv = ku + SG_W // SG_GC
    return pl.pallas_call(
        _sg_body,
        grid=(m // tt, SG_GROUPS),
        in_specs=[pl.BlockSpec((tt, SG_GC), lambda i, g: (i, ku + g)),
                  pl.BlockSpec((tt, SG_GC), lambda i, g: (i, kv + g)),
                  pl.BlockSpec((1, SG_GC), lambda i, g: (0, g)),
                  pl.BlockSpec((1, SG_GC), lambda i, g: (0, g)),
                  pl.BlockSpec((1, SG_CHUNK, SG_CHUNK), lambda i, g: (g, 0, 0)),
                  pl.BlockSpec((1, SG_CHUNK, 1), lambda i, g: (g, 0, 0))],
        out_specs=pl.BlockSpec((tt, SG_GC), lambda i, g: (i, g)),
        out_shape=jax.ShapeDtypeStruct((m, SG_W), BF),
        compiler_params=_cparams("parallel", "arbitrary"),
        name="spatial_gate",
    )(p, p, nw, nb, ws, bias_col)


def _mm_res2_body(a_ref, b_ref, wa_ref, wb_ref, r_ref, g_ref, o_ref):
    acc = (jnp.dot(a_ref[...], wa_ref[...], preferred_element_type=F32)
           + jnp.dot(b_ref[...], wb_ref[...], preferred_element_type=F32))
    o_ref[...] = r_ref[...] + g_ref[0] * acc


def _mm_res2(a, b, w, res, gate, rows_per_group, tm, tn):
    m, ka = a.shape
    kb = b.shape[1]
    n = w.shape[1]
    g = gate.shape[0]
    per = rows_per_group // tm
    assert ka == kb and w.shape[0] == ka + kb and m % tm == 0 and n % tn == 0 and rows_per_group % tm == 0
    return pl.pallas_call(
        _mm_res2_body,
        grid=(m // tm, n // tn),
        in_specs=[pl.BlockSpec((tm, ka), lambda i, j: (i, 0)),
                  pl.BlockSpec((tm, kb), lambda i, j: (i, 0)),
                  pl.BlockSpec((ka, tn), lambda i, j: (0, j)),
                  pl.BlockSpec((kb, tn), lambda i, j: (1, j)),
                  pl.BlockSpec((tm, tn), lambda i, j: (i, j)),
                  pl.BlockSpec((1, 1, tn), lambda i, j: (jnp.minimum(i // per, g - 1), 0, j))],
        out_specs=pl.BlockSpec((tm, tn), lambda i, j: (i, j)),
        out_shape=jax.ShapeDtypeStruct((m, n), F32),
        compiler_params=_cparams("parallel", "arbitrary"),
        name="mm_res2",
    )(a, b, w, w, res, gate)


def _axial_rope_tables(T):
    rows = T // GRID_W
    row, col = jnp.meshgrid(jnp.arange(rows), jnp.arange(GRID_W), indexing='ij')
    inv_freq = ROPE_BASE ** (-jnp.arange(0, AXIS_DIM, 2, dtype=jnp.float32) / AXIS_DIM)

    def angles(pos):
        ang = pos.reshape(-1).astype(jnp.float32)[:, None] * inv_freq
        return jnp.concatenate([ang, ang], axis=-1)

    ang = jnp.concatenate([angles(row), angles(col)], axis=-1)
    return jnp.cos(ang), jnp.sin(ang)


def _moe(hall, groups, gain, shift, scale, gate, rw_t, wg, wu, wd):
    m, d = hall.shape
    ne = rw_t.shape[0]
    set_len0 = groups[0][2]
    hm, aff = _router(hall, gain, shift, scale, rw_t, set_len0, 512)
    idx_parts, gate_parts, meta = [], [], []
    r_off = 0
    for row_off, n_sets, set_len in groups:
        cap = EC_CAPACITY * set_len // ne
        a = aff[:, row_off:row_off + n_sets * set_len].reshape(ne, n_sets, set_len)
        gts, idx = lax.top_k(a, cap)
        rows = idx + (row_off + jnp.arange(n_sets, dtype=jnp.int32) * set_len)[None, :, None]
        idx_parts.append(rows.reshape(ne, n_sets * cap))
        gate_parts.append(gts.reshape(ne, n_sets * cap))
        meta.append((row_off, n_sets, set_len, cap, r_off, idx))
        r_off += n_sets * cap
    rows_per_expert = r_off
    idx_all = jnp.concatenate(idx_parts, axis=1).astype(jnp.int32)
    gates_all = jnp.concatenate(gate_parts, axis=1)
    y = _experts(hm, idx_all.reshape(ne, 1, rows_per_expert), gates_all.reshape(ne, rows_per_expert, 1),
                 wg, wu, wd)
    y2d = y.reshape(ne * rows_per_expert, d)
    e_base = (jnp.arange(ne, dtype=jnp.int32) * rows_per_expert)[None, :, None]
    for gi, (row_off, n_sets, set_len, cap, r0, idx) in enumerate(meta):
        tok = jnp.swapaxes(idx, 0, 1).astype(jnp.int32)
        yrow = e_base + r0 + (jnp.arange(n_sets, dtype=jnp.int32) * cap)[:, None, None] \
            + jnp.arange(cap, dtype=jnp.int32)[None, None, :]
        if gi == 0:
            tabs = _combine_tables(tok.reshape(n_sets, ne * cap), yrow.reshape(n_sets, ne * cap), set_len)
            for s in range(n_sets):
                hall = _combine(hall, y2d, gate[s:s + 1], *(t[s] for t in tabs), row_off + s * set_len, set_len, ne)
        else:
            tok = tok + (jnp.arange(n_sets, dtype=jnp.int32) * set_len)[:, None, None]
            tabs = _combine_tables(tok.reshape(1, -1), yrow.reshape(1, -1), n_sets * set_len)
            hall = _combine(hall, y2d, gate[-1:], *(t[0] for t in tabs), row_off, n_sets * set_len, ne)
    return hall


def kernel(x, c, ctx, c_ctx, mod_w, mod_b, norm_mix, norm_ffn, ab_w_in, gla_gate_w, gla_gate_b, gla_norm,
           sg_norm_w, sg_norm_b, sg_ws, sg_bs, ab_w_out, c_w_in, sinks, c_w_out, router_w, moe_w_gate,
           moe_w_up, moe_w_down, final_norm):
    B, T, D = x.shape
    L = ctx.shape[1]
    ML, MC = B * T, B * L
    cos, sin = _axial_rope_tables(T)

    cond = jnp.concatenate([jax.nn.silu(c), jax.nn.silu(c_ctx)[None], jnp.zeros((3, D), x.dtype)], axis=0)
    hall = jnp.concatenate([x.reshape(ML, D), ctx.reshape(MC, D)], axis=0)

    for layer in range(DEPTH):
        last = layer == DEPTH - 1
        mods = _mm_bias(cond.astype(BF), mod_w[layer], mod_b[layer].reshape(1, N_MOD * D), 512)[:B + 1]
        mods = mods.reshape(B + 1, N_MOD, 1, D)
        mod = lambda i: mods[:, i]

        xall = _norm_mod(hall, norm_mix[layer], mod(0), mod(1), T, 512, BF)

        if layer % 2 == 0:
            e = layer // 2
            w_in = ab_w_in[e]
            w_main = jnp.concatenate([w_in[:, :6144], w_in[:, 6176:]], axis=1).astype(BF)
            w_lr = jnp.pad(w_in[:, 6144:6176], ((0, 0), (0, 96))).astype(BF)
            p = _mm(xall, w_main, 1024, 1024)
            lr = _mm(xall, w_lr, 1024, LANES)
            gw = [jnp.zeros((LANES, GLA_K_W), F32).at[d * GLA_RANK:(d + 1) * GLA_RANK].set(gla_gate_w[e, d]).astype(BF)
                  for d in range(2)]
            o_fwd = _gla(p, lr, gw[0], gla_gate_b[e, 0:1], None, None, B, T, L, reverse=False)
            a_mix = _gla(p, lr, gw[1], gla_gate_b[e, 1:2], o_fwd, gla_norm[e].reshape(1, GLA_V_W), B, T, L,
                         reverse=True)
            b_mix = _spatial_gate(p, sg_norm_w[e].reshape(1, SG_W), sg_norm_b[e].reshape(1, SG_W), sg_ws[e],
                                  sg_bs[e].reshape(SG_GROUPS, SG_CHUNK, 1))
            hall = _mm_res2(a_mix, b_mix, ab_w_out[e].astype(BF), hall, mod(2), T, 1024, 1024)
        else:
            o = layer // 2
            p = _mm(xall, c_w_in[o].astype(BF), 1024, 1024)
            att = _attention(p, cos, sin, sinks[o], B, T, L)
            hall = _mm_res(att, c_w_out[o].astype(BF), hall[:ML], mods[:B, 2], T, 1024, 1024)

        rw_t = router_w[layer].T.astype(BF)
        wg, wu, wd = moe_w_gate[layer].astype(BF), moe_w_up[layer].astype(BF), moe_w_down[layer].astype(BF)
        if last:
            hall = _moe(hall, [(0, B, T)], norm_ffn[layer], mods[:B, 3], mods[:B, 4], mods[:B, 5], rw_t, wg, wu, wd)
        else:
            hall = _moe(hall, [(0, B, T), (ML, B, L)], norm_ffn[layer], mod(3), mod(4), mod(5), rw_t, wg, wu, wd)

    zero = jnp.zeros((1, 1, D), jnp.float32)
    out = _norm_mod(hall[:ML], final_norm, zero, zero, ML, 512, jnp.float32)
    return out.reshape(B, T, D)
```

```python
import functools

import jax
import jax.numpy as jnp
import numpy as np
from jax import lax
from jax.experimental import pallas as pl
from jax.experimental.pallas import tpu as pltpu

D_MODEL = 4096
DEPTH = 2
GRID_W = 64
EPS = 1e-6
NEG_INF = -1e30
N_MOD = 6
MIX_W = D_MODEL
GLA_HEADS = 4
GLA_V_W = MIX_W // 2
GLA_DV = GLA_V_W // GLA_HEADS
GLA_K_W = GLA_V_W // 2
GLA_DK = GLA_K_W // GLA_HEADS
GLA_RANK = 16
GLA_TAU = 16.0
GLA_CHUNK = 64
SG_GROUPS = 4
SG_W = MIX_W // 2
SG_GC = SG_W // SG_GROUPS
SG_CHUNK = 128
ATT_HEADS = D_MODEL // 128
KV_HEADS = 8
HEAD_DIM = 128
GQ = ATT_HEADS // KV_HEADS
WINDOW = 128
ATT_BLOCK = 128
AXIS_DIM = HEAD_DIM // 2
ROPE_BASE = 10000.0
Q_W = ATT_HEADS * HEAD_DIM
KV_W = KV_HEADS * HEAD_DIM
N_EXPERTS = 16
EC_CAPACITY = 2

V7X_VMEM_LIMIT_BYTES = 56 * 1024 * 1024
LANES = 128

BF = jnp.bfloat16
F32 = jnp.float32


def _cparams(*sem):
    return pltpu.CompilerParams(dimension_semantics=sem, vmem_limit_bytes=V7X_VMEM_LIMIT_BYTES)


def _mm_body(x_ref, w_ref, o_ref):
    o_ref[...] = jnp.dot(x_ref[...], w_ref[...], preferred_element_type=F32).astype(o_ref.dtype)


def _mm(x, w, tm, tn, out_dtype=F32):
    m, k = x.shape
    n = w.shape[1]
    assert m % tm == 0 and n % tn == 0, (m, n, tm, tn)
    return pl.pallas_call(
        _mm_body,
        grid=(m // tm, n // tn),
        in_specs=[pl.BlockSpec((tm, k), lambda i, j: (i, 0)),
                  pl.BlockSpec((k, tn), lambda i, j: (0, j))],
        out_specs=pl.BlockSpec((tm, tn), lambda i, j: (i, j)),
        out_shape=jax.ShapeDtypeStruct((m, n), out_dtype),
        compiler_params=_cparams("parallel", "arbitrary"),
        name="mm",
    )(x, w)


def _mm_bias_body(x_ref, w_ref, b_ref, o_ref):
    o_ref[...] = jnp.dot(x_ref[...], w_ref[...].astype(BF), preferred_element_type=F32) + b_ref[...]


def _mm_bias(x, w, layer, bias, tn):
    m, k = x.shape
    n = w.shape[2]
    assert n % tn == 0
    return pl.pallas_call(
        _mm_bias_body,
        grid=(n // tn,),
        in_specs=[pl.BlockSpec((m, k), lambda j: (0, 0)),
                  pl.BlockSpec((None, k, tn), lambda j: (layer, 0, j)),
                  pl.BlockSpec((1, tn), lambda j: (0, j))],
        out_specs=pl.BlockSpec((m, tn), lambda j: (0, j)),
        out_shape=jax.ShapeDtypeStruct((m, n), F32),
        compiler_params=_cparams("parallel"),
        name="mm_bias",
    )(x, w, bias)


def _mm_res_body(x_ref, w_ref, r_ref, g_ref, o_ref):
    acc = jnp.dot(x_ref[...], w_ref[...], preferred_element_type=F32)
    o_ref[...] = r_ref[...] + g_ref[0] * acc


def _mm_res(x, w, res, gate, rows_per_group, tm, tn):
    m, k = x.shape
    n = w.shape[1]
    g = gate.shape[0]
    per = rows_per_group // tm
    assert m % tm == 0 and n % tn == 0 and rows_per_group % tm == 0
    return pl.pallas_call(
        _mm_res_body,
        grid=(m // tm, n // tn),
        in_specs=[pl.BlockSpec((tm, k), lambda i, j: (i, 0)),
                  pl.BlockSpec((k, tn), lambda i, j: (0, j)),
                  pl.BlockSpec((tm, tn), lambda i, j: (i, j)),
                  pl.BlockSpec((1, 1, tn), lambda i, j: (jnp.minimum(i // per, g - 1), 0, j))],
        out_specs=pl.BlockSpec((tm, tn), lambda i, j: (i, j)),
        out_shape=jax.ShapeDtypeStruct((m, n), F32),
        compiler_params=_cparams("parallel", "arbitrary"),
        name="mm_res",
    )(x, w, res, gate)


def _modulated(x, gain, shift, scale):
    y = x * lax.rsqrt(jnp.mean(x * x, axis=-1, keepdims=True) + EPS)
    return (y * gain) * (1.0 + scale) + shift


def _norm_mod_body(x_ref, g_ref, sh_ref, sc_ref, o_ref):
    o_ref[...] = _modulated(x_ref[...], g_ref[...], sh_ref[0], sc_ref[0]).astype(o_ref.dtype)


def _group_index(rows_per_group, tm, groups):
    per = rows_per_group // tm
    assert rows_per_group % tm == 0
    return lambda i: (jnp.minimum(i // per, groups - 1), 0, 0)


def _norm_mod(h, gain, shift, scale, rows_per_group, tm, out_dtype):
    m, d = h.shape
    grp = _group_index(rows_per_group, tm, shift.shape[0])
    return pl.pallas_call(
        _norm_mod_body,
        grid=(m // tm,),
        in_specs=[pl.BlockSpec((tm, d), lambda i: (i, 0)),
                  pl.BlockSpec((1, d), lambda i: (0, 0)),
                  pl.BlockSpec((1, 1, d), grp),
                  pl.BlockSpec((1, 1, d), grp)],
        out_specs=pl.BlockSpec((tm, d), lambda i: (i, 0)),
        out_shape=jax.ShapeDtypeStruct((m, d), out_dtype),
        compiler_params=_cparams("parallel"),
        name="norm_mod",
    )(h, gain.reshape(1, d), shift, scale)


def _router_body(x_ref, g_ref, sh_ref, sc_ref, rwt_ref, hm_ref, aff_ref):
    hm = _modulated(x_ref[...], g_ref[...], sh_ref[0], sc_ref[0])
    hm_ref[...] = hm
    logits = lax.dot_general(rwt_ref[...], hm.astype(BF), (((1,), (1,)), ((), ())),
                             preferred_element_type=F32)
    e = jnp.exp(logits - jnp.max(logits, axis=0, keepdims=True))
    aff_ref[...] = e / jnp.sum(e, axis=0, keepdims=True)


def _router(h, gain, shift, scale, rw_t, rows_per_group, tm):
    m, d = h.shape
    ne = rw_t.shape[0]
    grp = _group_index(rows_per_group, tm, shift.shape[0])
    return pl.pallas_call(
        _router_body,
        grid=(m // tm,),
        in_specs=[pl.BlockSpec((tm, d), lambda i: (i, 0)),
                  pl.BlockSpec((1, d), lambda i: (0, 0)),
                  pl.BlockSpec((1, 1, d), grp),
                  pl.BlockSpec((1, 1, d), grp),
                  pl.BlockSpec((ne, d), lambda i: (0, 0))],
        out_specs=[pl.BlockSpec((tm, d), lambda i: (i, 0)),
                   pl.BlockSpec((ne, tm), lambda i: (0, i))],
        out_shape=[jax.ShapeDtypeStruct((m, d), F32), jax.ShapeDtypeStruct((ne, m), F32)],
        compiler_params=_cparams("parallel"),
        name="router",
    )(h, gain.reshape(1, d), shift, scale, rw_t)


GATHER_ROWS = 128
GATHER_BUFS = 3
EXPERT_K_SPLIT = 2


def _experts_body(idx_vm, idx_next_vm, gates_ref, hm_hbm, wg_ref, wu_ref, wd_ref, y_ref,
                  idx_sm, xg, hid, g_acc, u_acc, stage, sem, *, rows, mm_rows, n_ff_tiles, n_d_tiles):
    e = pl.program_id(0)
    j = pl.program_id(1)
    n_experts = pl.num_programs(0)
    ch, nb, ks = GATHER_ROWS, GATHER_BUFS, EXPERT_K_SPLIT
    n_chunks = rows // ch
    tf = hid.shape[2]
    kd = xg.shape[2]
    n_up, n_down = n_ff_tiles * ks, n_d_tiles * ks

    def issue(q):
        def body(r, carry):
            src = idx_sm[0, q * ch + r]
            pltpu.make_async_copy(hm_hbm.at[pl.ds(src, 1)], stage.at[q % nb, pl.ds(r, 1)], sem.at[q % nb]).start()
            return carry
        lax.fori_loop(0, ch, body, 0, unroll=8)

    def finish(q):
        pltpu.make_async_copy(hm_hbm.at[pl.ds(0, ch)], stage.at[q % nb], sem.at[q % nb]).wait()
        for kh in range(ks):
            xg[kh, q * ch:(q + 1) * ch, :] = stage[q % nb, :, kh * kd:(kh + 1) * kd].astype(BF)

    def gather(lo, hi, first):
        if first:
            for q in range(min(nb, n_chunks)):
                issue(q)
        for q in range(lo, hi):
            finish(q)
            if q + nb < n_chunks:
                issue(q + nb)

    @pl.when((e == 0) & (j == 0))
    def _first_gather():
        pltpu.sync_copy(idx_vm.at[0], idx_sm)
        gather(0, n_chunks, True)

    for k in range(n_down):
        lo, hi = k * n_chunks // n_down, (k + 1) * n_chunks // n_down
        if k == 0 or hi > lo:
            @pl.when((j == n_up + k) & (e + 1 < n_experts))
            def _next_gather(k=k, lo=lo, hi=hi):
                if k == 0:
                    pltpu.sync_copy(idx_next_vm.at[0], idx_sm)
                gather(lo, hi, k == 0)

    @pl.when(j < n_up)
    def _up():
        f, kh = j // ks, j % ks
        wg = wg_ref[0, 0, 0].astype(BF)
        wu = wu_ref[0, 0, 0].astype(BF)
        for rc in range(rows // mm_rows):
            rs = slice(rc * mm_rows, (rc + 1) * mm_rows)
            xs = xg[kh, rs, :]
            g = jnp.dot(xs, wg, preferred_element_type=F32)
            u = jnp.dot(xs, wu, preferred_element_type=F32)

            @pl.when(kh == 0)
            def _start():
                g_acc[rs, :] = g
                u_acc[rs, :] = u

            @pl.when((kh > 0) & (kh < ks - 1))
            def _accumulate():
                g_acc[rs, :] += g
                u_acc[rs, :] += u

            @pl.when(kh == ks - 1)
            def _finish():
                gt = g_acc[rs, :] + g
                ut = u_acc[rs, :] + u
                hid[f, rs, :] = ((gt * jax.nn.sigmoid(gt)) * ut).astype(BF)

    @pl.when(j >= n_up)
    def _down():
        kh = (j - n_up) % ks
        per = n_ff_tiles // ks
        wd = wd_ref[0, 0, 0].astype(BF)
        for rc in range(rows // mm_rows):
            rs = slice(rc * mm_rows, (rc + 1) * mm_rows)
            acc = jnp.dot(hid[kh * per, rs, :], wd[0:tf, :], preferred_element_type=F32)
            for f in range(1, per):
                acc = acc + jnp.dot(hid[kh * per + f, rs, :], wd[f * tf:(f + 1) * tf, :], preferred_element_type=F32)
            acc = acc * gates_ref[0, rs, :]

            @pl.when(kh == 0)
            def _start():
                y_ref[0, rs, :] = acc

            @pl.when(kh > 0)
            def _accumulate():
                y_ref[0, rs, :] += acc


def _experts(hm, idx, gates, wg, wu, wd, layer, tf=256, td=512):
    ks = EXPERT_K_SPLIT
    ne, _, rows = idx.shape
    d = hm.shape[1]
    ff = wg.shape[3]
    n_ff_tiles, n_d_tiles = ff // tf, d // td
    n_up, n_down = n_ff_tiles * ks, n_d_tiles * ks
    mm_rows = rows // 4
    assert rows % GATHER_ROWS == 0 and mm_rows % 8 == 0 and n_ff_tiles % ks == 0 and d % ks == 0
    body = functools.partial(_experts_body, rows=rows, mm_rows=mm_rows, n_ff_tiles=n_ff_tiles, n_d_tiles=n_d_tiles)
    up = lambda j: jnp.minimum(j, n_up - 1)
    down = lambda j: jnp.maximum(j - n_up, 0)
    wg4 = wg.reshape(wg.shape[0], ne, ks, d // ks, ff)
    wu4 = wu.reshape(wu.shape[0], ne, ks, d // ks, ff)
    wd4 = wd.reshape(wd.shape[0], ne, ks, ff // ks, d)
    up_spec = pl.BlockSpec((1, 1, 1, d // ks, tf), lambda e, j: (layer, e, up(j) % ks, 0, up(j) // ks))
    return pl.pallas_call(
        body,
        grid=(ne, n_up + n_down),
        in_specs=[pl.BlockSpec((1, 1, rows), lambda e, j: (e, 0, 0)),
                  pl.BlockSpec((1, 1, rows), lambda e, j: (jnp.minimum(e + 1, ne - 1), 0, 0)),
                  pl.BlockSpec((1, rows, 1), lambda e, j: (e, 0, 0)),
                  pl.BlockSpec(memory_space=pl.ANY),
                  up_spec,
                  up_spec,
                  pl.BlockSpec((1, 1, 1, ff // ks, td), lambda e, j: (layer, e, down(j) % ks, 0, down(j) // ks))],
        out_specs=pl.BlockSpec((1, rows, td), lambda e, j: (e, 0, down(j) // ks)),
        out_shape=jax.ShapeDtypeStruct((ne, rows, d), F32),
        scratch_shapes=[pltpu.SMEM((1, rows), jnp.int32),
                        pltpu.VMEM((ks, rows, d // ks), BF),
                        pltpu.VMEM((n_ff_tiles, rows, tf), BF),
                        pltpu.VMEM((rows, tf), F32),
                        pltpu.VMEM((rows, tf), F32),
                        pltpu.VMEM((GATHER_BUFS, GATHER_ROWS, d), F32),
                        pltpu.SemaphoreType.DMA((GATHER_BUFS,))],
        compiler_params=_cparams("arbitrary", "arbitrary"),
        name="experts",
    )(idx, idx, gates, hm, wg4, wu4, wd4)


COMBINE_TOKENS = 32
COMBINE_BUFS = 4
COMBINE_STRIP = 512
COMBINE_UNROLL = 4


def _combine_body(pk_ref, ts_ref, mc_ref, h_ref, cnt_ref, gate_ref, y_hbm, o_ref, buf, sem, *, n_tiles, n_slots):
    g = pl.program_id(0)
    tt, d = h_ref.shape
    nb = buf.shape[0]

    def issue(t):
        slot = t % nb

        def start(p):
            v = pk_ref[p]
            src = jnp.bitwise_and(v, 0xFFFF)
            dst = jnp.right_shift(v, 16)
            pltpu.make_async_copy(y_hbm.at[pl.ds(src, 1)], buf.at[slot, pl.ds(dst, 1)], sem.at[slot]).start()

        lo, hi = ts_ref[t], ts_ref[t + 1]
        n_groups = (hi - lo) // COMBINE_UNROLL

        def group(i, carry):
            for u in range(COMBINE_UNROLL):
                start(lo + i * COMBINE_UNROLL + u)
            return carry
        lax.fori_loop(0, n_groups, group, 0)

        def single(p, carry):
            start(p)
            return carry
        lax.fori_loop(lo + n_groups * COMBINE_UNROLL, hi, single, 0)

    def drain(t):
        slot = t % nb
        n = ts_ref[t + 1] - ts_ref[t]
        for bit in range((n_slots * tt).bit_length()):
            @pl.when(jnp.bitwise_and(n, 1 << bit) != 0)
            def _wait(bit=bit):
                rows = pl.ds(0, 1 << bit)
                pltpu.make_async_copy(y_hbm.at[rows], buf.at[slot, rows], sem.at[slot]).wait()

    @pl.when(g == 0)
    def _first():
        def zero(r, carry):
            buf[:, pl.ds(pl.multiple_of(r * tt, tt), tt), :] = jnp.zeros((nb, tt, d), F32)
            return carry
        lax.fori_loop(0, n_slots, zero, 0)
        for t in range(min(nb - 1, n_tiles)):
            issue(t)

    @pl.when(g + nb - 1 < n_tiles)
    def _prefetch():
        issue(g + nb - 1)

    slot = g % nb
    drain(g)
    n_used = mc_ref[g]
    cnt = jnp.broadcast_to(cnt_ref[...], (tt, COMBINE_STRIP))
    for cs in range(d // COMBINE_STRIP):
        cols = slice(cs * COMBINE_STRIP, (cs + 1) * COMBINE_STRIP)

        def body(k, acc):
            v = buf[slot, pl.ds(pl.multiple_of(k * tt, tt), tt), cols]
            return acc + jnp.where(cnt > k, v, 0.0)

        acc = lax.fori_loop(0, n_used, body, jnp.zeros((tt, COMBINE_STRIP), F32))
        o_ref[:, cols] = h_ref[:, cols] + gate_ref[0][:, cols] * acc


def _combine_tables(tok, yrow, n_tok):
    tt = COMBINE_TOKENS
    s, n_pairs = tok.shape
    key = jnp.sort(jnp.bitwise_or(jnp.left_shift(tok, 16), yrow), axis=-1)
    tok_s = jnp.right_shift(key, 16)
    src = jnp.bitwise_and(key, 0xFFFF)
    cnt = jnp.sum(tok[:, :, None] == jnp.arange(n_tok, dtype=jnp.int32)[None, None, :], axis=1, dtype=jnp.int32)
    start = jnp.concatenate([jnp.zeros((s, 1), jnp.int32), jnp.cumsum(cnt, axis=1, dtype=jnp.int32)], axis=1)
    pos = jnp.broadcast_to(jnp.arange(n_pairs, dtype=jnp.int32), (s, n_pairs))
    is_first = jnp.concatenate([jnp.ones((s, 1), bool), tok_s[:, 1:] != tok_s[:, :-1]], axis=1)
    k = pos - lax.cummax(jnp.where(is_first, pos, 0), axis=1)
    packed = jnp.bitwise_or(src, jnp.left_shift(k * tt + tok_s % tt, 16))
    return packed, start[:, ::tt], jnp.max(cnt.reshape(s, n_tok // tt, tt), axis=2), cnt


def _combine(h, y2d, gate, packed, tile_start, tile_slots, cnt, row_off, rows_per_group, n_slots):
    tt = COMBINE_TOKENS
    d = h.shape[1]
    n_tok = cnt.shape[0]
    n_tiles = n_tok // tt
    assert n_tok % tt == 0 and row_off % tt == 0 and y2d.shape[0] <= 0x10000 and n_slots * tt <= 0x7FFF
    tile0 = row_off // tt
    per = rows_per_group // tt
    n_groups = gate.shape[0]
    grid_spec = pltpu.PrefetchScalarGridSpec(
        num_scalar_prefetch=3,
        grid=(n_tiles,),
        in_specs=[pl.BlockSpec((tt, d), lambda g, *_: (tile0 + g, 0)),
                  pl.BlockSpec((tt, 1), lambda g, *_: (g, 0)),
                  pl.BlockSpec((1, 1, d), lambda g, *_: (jnp.minimum(g // per, n_groups - 1), 0, 0)),
                  pl.BlockSpec(memory_space=pl.ANY)],
        out_specs=pl.BlockSpec((tt, d), lambda g, *_: (tile0 + g, 0)),
        scratch_shapes=[pltpu.VMEM((COMBINE_BUFS, n_slots * tt, d), F32),
                        pltpu.SemaphoreType.DMA((COMBINE_BUFS,))],
    )
    return pl.pallas_call(
        functools.partial(_combine_body, n_tiles=n_tiles, n_slots=n_slots),
        grid_spec=grid_spec,
        out_shape=jax.ShapeDtypeStruct(h.shape, F32),
        input_output_aliases={3: 0},
        compiler_params=_cparams("arbitrary"),
        name="combine",
    )(packed, tile_start, tile_slots, h, cnt.reshape(n_tok, 1), gate, y2d)


ATT_Q_TILE = 512


def _rope(x, cos, sin):
    lane = lax.broadcasted_iota(jnp.int32, x.shape, 1)
    fwd = pltpu.roll(x, AXIS_DIM // 2, axis=1)
    bwd = pltpu.roll(x, HEAD_DIM - AXIS_DIM // 2, axis=1)
    rot = jnp.where(lane % AXIS_DIM < AXIS_DIM // 2, -bwd, fwd)
    return x * cos + rot * sin


def _attn_body(q_ref, kp_ref, km_ref, kn_ref, vp_ref, vm_ref, vn_ref, kc_ref, vc_ref,
               cm_ref, sm_ref, cp_ref, sp_ref, cn_ref, sn_ref, sink_ref, o_ref, *, n_q_tiles):
    i = pl.program_id(2)
    qt = q_ref.shape[0]
    blk = ATT_BLOCK
    nb = qt // blk
    scale = HEAD_DIM ** -0.5
    cos_w = jnp.concatenate([cp_ref[...], cm_ref[...], cn_ref[...]], axis=0)
    sin_w = jnp.concatenate([sp_ref[...], sm_ref[...], sn_ref[...]], axis=0)
    k_w = jnp.concatenate([kp_ref[...], km_ref[...], kn_ref[...]], axis=0)
    k_w = _rope(k_w, cos_w, sin_w).astype(BF)
    v_w = jnp.concatenate([vp_ref[...], vm_ref[...], vn_ref[...]], axis=0).astype(BF)
    kc = kc_ref[...].astype(BF)
    vc = vc_ref[...].astype(BF)
    sink = sink_ref[0]
    nt = (((1,), (1,)), ((), ()))
    band = 3 * blk
    row = lax.broadcasted_iota(jnp.int32, (GQ * blk, band), 0) % blk
    col = lax.broadcasted_iota(jnp.int32, (GQ * blk, band), 1)
    in_window = (col >= row) & (col <= row + 2 * WINDOW)
    for jb in range(nb):
        rs = slice(jb * blk, (jb + 1) * blk)
        cos_q = cm_ref[rs, :]
        sin_q = sm_ref[rs, :]
        qs = jnp.concatenate([_rope(q_ref[rs, g * HEAD_DIM:(g + 1) * HEAD_DIM], cos_q, sin_q)
                              for g in range(GQ)], axis=0).astype(BF)
        kb = k_w[jb * blk:jb * blk + band]
        vb = v_w[jb * blk:jb * blk + band]
        s_loc = lax.dot_general(qs, kb, nt, preferred_element_type=F32) * scale
        s_ctx = lax.dot_general(qs, kc, nt, preferred_element_type=F32) * scale
        valid = in_window
        if jb == 0:
            valid = valid & (col >= jnp.where(i == 0, blk, 0))
        if jb == nb - 1:
            valid = valid & (col < jnp.where(i == n_q_tiles - 1, 2 * blk, band))
        s_loc = jnp.where(valid, s_loc, NEG_INF)
        m = jnp.maximum(jnp.maximum(jnp.max(s_loc, axis=1, keepdims=True),
                                    jnp.max(s_ctx, axis=1, keepdims=True)), sink)
        e_loc = jnp.exp(s_loc - m)
        e_ctx = jnp.exp(s_ctx - m)
        den = (jnp.sum(e_loc, axis=1, keepdims=True) + jnp.sum(e_ctx, axis=1, keepdims=True)
               + jnp.exp(sink - m))
        o = (jnp.dot(e_loc.astype(BF), vb, preferred_element_type=F32)
             + jnp.dot(e_ctx.astype(BF), vc, preferred_element_type=F32)) / den
        for g in range(GQ):
            o_ref[rs, g * HEAD_DIM:(g + 1) * HEAD_DIM] = o[g * blk:(g + 1) * blk].astype(o_ref.dtype)


def _attention(p, cos, sin, sinks, n_batch, seq, ctx_len):
    qt, blk = ATT_Q_TILE, ATT_BLOCK
    ml = n_batch * seq
    nq = seq // qt
    nbk = seq // blk
    bpt = qt // blk
    kcol, vcol = Q_W // HEAD_DIM, (Q_W + KV_W) // HEAD_DIM
    assert seq % qt == 0 and ml % ctx_len == 0
    sink_col = jnp.repeat(sinks.astype(F32).reshape(KV_HEADS, GQ, 1), blk, axis=2).reshape(KV_HEADS, GQ * blk, 1)
    prev_blk = lambda i: jnp.maximum(i * bpt - 1, 0)
    next_blk = lambda i: jnp.minimum((i + 1) * bpt, nbk - 1)
    qspec = pl.BlockSpec((qt, GQ * HEAD_DIM), lambda b, h, i: (b * nq + i, h))

    def kv_specs(c0):
        return [pl.BlockSpec((blk, HEAD_DIM), lambda b, h, i: (b * nbk + prev_blk(i), c0 + h)),
                pl.BlockSpec((qt, HEAD_DIM), lambda b, h, i: (b * nq + i, c0 + h)),
                pl.BlockSpec((blk, HEAD_DIM), lambda b, h, i: (b * nbk + next_blk(i), c0 + h))]

    ctx_spec = lambda c0: pl.BlockSpec((ctx_len, HEAD_DIM), lambda b, h, i: (ml // ctx_len + b, c0 + h))
    tab_specs = [pl.BlockSpec((qt, HEAD_DIM), lambda b, h, i: (i, 0)),
                 pl.BlockSpec((qt, HEAD_DIM), lambda b, h, i: (i, 0)),
                 pl.BlockSpec((blk, HEAD_DIM), lambda b, h, i: (prev_blk(i), 0)),
                 pl.BlockSpec((blk, HEAD_DIM), lambda b, h, i: (prev_blk(i), 0)),
                 pl.BlockSpec((blk, HEAD_DIM), lambda b, h, i: (next_blk(i), 0)),
                 pl.BlockSpec((blk, HEAD_DIM), lambda b, h, i: (next_blk(i), 0))]
    return pl.pallas_call(
        functools.partial(_attn_body, n_q_tiles=nq),
        grid=(n_batch, KV_HEADS, nq),
        in_specs=[qspec] + kv_specs(kcol) + kv_specs(vcol) + [ctx_spec(kcol), ctx_spec(vcol)] + tab_specs
                 + [pl.BlockSpec((1, GQ * blk, 1), lambda b, h, i: (h, 0, 0))],
        out_specs=pl.BlockSpec((qt, GQ * HEAD_DIM), lambda b, h, i: (b * nq + i, h)),
        out_shape=jax.ShapeDtypeStruct((ml, Q_W), BF),
        compiler_params=_cparams("parallel", "parallel", "arbitrary"),
        name="attention",
    )(p, p, p, p, p, p, p, p, p, cos, sin, cos, sin, cos, sin, sink_col)


GLA_TOKENS = 256
_NT = (((1,), (1,)), ((), ()))
_TN = (((0,), (0,)), ((), ()))


def _gla_body(*refs, reverse, fuse_output):
    if fuse_output:
        q_ref, k_ref, v_ref, lr_ref, gw_ref, gb_ref, of_ref, g_ref, gain_ref, o_ref, st_ref = refs
    else:
        q_ref, k_ref, v_ref, lr_ref, gw_ref, gb_ref, o_ref, st_ref = refs
    tb = q_ref.shape[0]
    ch = GLA_CHUNK
    n_ch = tb // ch

    @pl.when(pl.program_id(2) == 0)
    def _reset():
        st_ref[...] = jnp.zeros(st_ref.shape, F32)

    r = lax.broadcasted_iota(jnp.int32, (tb, tb), 0)
    c = lax.broadcasted_iota(jnp.int32, (tb, tb), 1)
    same = (r // ch) == (c // ch)
    tri = same & ((c >= r) if reverse else (c <= r))
    tri_bf = jnp.where(tri, 1.0, 0.0).astype(BF)
    same_bf = jnp.where(same, 1.0, 0.0).astype(BF)

    z = jnp.dot(lr_ref[...].astype(BF), gw_ref[...], preferred_element_type=F32) + gb_ref[...]
    la = jax.nn.log_sigmoid(z) * (1.0 / GLA_TAU)
    la_hi = la.astype(BF)
    la_lo = (la - la_hi.astype(F32)).astype(BF)
    b = (jnp.dot(tri_bf, la_hi, preferred_element_type=F32)
         + jnp.dot(tri_bf, la_lo, preferred_element_type=F32))
    tot = (jnp.dot(same_bf, la_hi, preferred_element_type=F32)
           + jnp.dot(same_bf, la_lo, preferred_element_type=F32))
    q = q_ref[...] * (GLA_DK ** -0.5)
    k = k_ref[...]
    v = v_ref[...].astype(BF)
    q_in = (q * jnp.exp(b)).astype(BF)
    k_in = (k * jnp.exp(-b)).astype(BF)
    k_dec = (k * jnp.exp(tot - b)).astype(BF)
    dec = jnp.exp(tot)
    sc = lax.dot_general(q_in, k_in, _NT, preferred_element_type=F32)
    sc = jnp.where(tri, sc, 0.0).astype(BF)
    o_intra = jnp.dot(sc, v, preferred_element_type=F32)
    outs = [None] * n_ch
    for ci in (range(n_ch - 1, -1, -1) if reverse else range(n_ch)):
        rs = slice(ci * ch, (ci + 1) * ch)
        st = st_ref[...]
        outs[ci] = o_intra[rs] + lax.dot_general(q_in[rs], st.astype(BF), _NT, preferred_element_type=F32)
        upd = lax.dot_general(v[rs], k_dec[rs], _TN, preferred_element_type=F32)
        st_ref[...] = st * dec[ci * ch:ci * ch + 1, :] + upd
    o = jnp.concatenate(outs, axis=0)
    if fuse_output:
        o = o + of_ref[...]
        y = o * lax.rsqrt(jnp.mean(o * o, axis=-1, keepdims=True) + EPS) * gain_ref[...]
        g = g_ref[...]
        o_ref[...] = (y * (g * jax.nn.sigmoid(g))).astype(o_ref.dtype)
    else:
        o_ref[...] = o


def _gla(p, lr, gate_w, gate_b, o_fwd, gain, n_batch, seq, ctx_len, *, reverse):
    tb = GLA_TOKENS
    ml = n_batch * seq
    m = p.shape[0]
    nc, nl = ctx_len // tb, seq // tb
    assert ctx_len % tb == 0 and seq % tb == 0 and ml % tb == 0
    ctx_base = ml // tb

    def row(b, s):
        sc_ = (nc - 1 - s) if reverse else s
        sl_ = (nl - 1 - (s - nc)) if reverse else (s - nc)
        return jnp.where(s < nc, ctx_base + b * nc + sc_, b * nl + sl_)

    kq, kk, kv, kg = 0, GLA_K_W // GLA_DK, 2 * GLA_K_W // GLA_DV, (2 * GLA_K_W + GLA_V_W) // GLA_DV
    in_specs = [pl.BlockSpec((tb, GLA_DK), lambda b, h, s: (row(b, s), kq + h)),
                pl.BlockSpec((tb, GLA_DK), lambda b, h, s: (row(b, s), kk + h)),
                pl.BlockSpec((tb, GLA_DV), lambda b, h, s: (row(b, s), kv + h)),
                pl.BlockSpec((tb, LANES), lambda b, h, s: (row(b, s), 0)),
                pl.BlockSpec((LANES, GLA_DK), lambda b, h, s: (0, h)),
                pl.BlockSpec((1, GLA_DK), lambda b, h, s: (0, h))]
    args = [p, p, p, lr, gate_w, gate_b]
    if reverse:
        in_specs += [pl.BlockSpec((tb, GLA_DV), lambda b, h, s: (row(b, s), h)),
                     pl.BlockSpec((tb, GLA_DV), lambda b, h, s: (row(b, s), kg + h)),
                     pl.BlockSpec((1, GLA_DV), lambda b, h, s: (0, h))]
        args += [o_fwd, p, gain]
    return pl.pallas_call(
        functools.partial(_gla_body, reverse=reverse, fuse_output=reverse),
        grid=(n_batch, GLA_HEADS, nc + nl),
        in_specs=in_specs,
        out_specs=pl.BlockSpec((tb, GLA_DV), lambda b, h, s: (row(b, s), h)),
        out_shape=jax.ShapeDtypeStruct((m, GLA_V_W), BF if reverse else F32),
        scratch_shapes=[pltpu.VMEM((GLA_DV, GLA_DK), F32)],
        compiler_params=_cparams("parallel", "parallel", "arbitrary"),
        name="gla_bwd" if reverse else "gla_fwd",
    )(*args)


SG_TOKENS = 512


def _sg_body(u_ref, vs_ref, nw_ref, nb_ref, ws_ref, bs_ref, o_ref):
    u = jax.nn.gelu(u_ref[...])
    vf = jax.nn.gelu(vs_ref[...])
    mu = jnp.mean(vf, axis=-1, keepdims=True)
    var = jnp.mean(jnp.square(vf - mu), axis=-1, keepdims=True)
    vn = ((vf - mu) * lax.rsqrt(var + EPS) * nw_ref[...] + nb_ref[...]).astype(BF)
    ws = ws_ref[0].astype(BF)
    bias = bs_ref[0]
    for ci in range(u.shape[0] // SG_CHUNK):
        rs = slice(ci * SG_CHUNK, (ci + 1) * SG_CHUNK)
        s = jnp.dot(ws, vn[rs], preferred_element_type=F32) + bias
        o_ref[rs, :] = (u[rs] * s).astype(o_ref.dtype)


def _spatial_gate(p, nw, nb, ws, bias_col):
    m = p.shape[0]
    tt = SG_TOKENS
    ku = (2 * GLA_K_W + 2 * GLA_V_W) // SG_GC
    kv = ku + SG_W // SG_GC
    return pl.pallas_call(
        _sg_body,
        grid=(m // tt, SG_GROUPS),
        in_specs=[pl.BlockSpec((tt, SG_GC), lambda i, g: (i, ku + g)),
                  pl.BlockSpec((tt, SG_GC), lambda i, g: (i, kv + g)),
                  pl.BlockSpec((1, SG_GC), lambda i, g: (0, g)),
                  pl.BlockSpec((1, SG_GC), lambda i, g: (0, g)),
                  pl.BlockSpec((1, SG_CHUNK, SG_CHUNK), lambda i, g: (g, 0, 0)),
                  pl.BlockSpec((1, SG_CHUNK, 1), lambda i, g: (g, 0, 0))],
        out_specs=pl.BlockSpec((tt, SG_GC), lambda i, g: (i, g)),
        out_shape=jax.ShapeDtypeStruct((m, SG_W), BF),
        compiler_params=_cparams("parallel", "arbitrary"),
        name="spatial_gate",
    )(p, p, nw, nb, ws, bias_col)


def _mm_res2_body(a_ref, b_ref, wa_ref, wb_ref, r_ref, g_ref, o_ref):
    acc = (jnp.dot(a_ref[...], wa_ref[...], preferred_element_type=F32)
           + jnp.dot(b_ref[...], wb_ref[...], preferred_element_type=F32))
    o_ref[...] = r_ref[...] + g_ref[0] * acc


def _mm_res2(a, b, w, res, gate, rows_per_group, tm, tn):
    m, ka = a.shape
    kb = b.shape[1]
    n = w.shape[1]
    g = gate.shape[0]
    per = rows_per_group // tm
    assert ka == kb and w.shape[0] == ka + kb and m % tm == 0 and n % tn == 0 and rows_per_group % tm == 0
    return pl.pallas_call(
        _mm_res2_body,
        grid=(m // tm, n // tn),
        in_specs=[pl.BlockSpec((tm, ka), lambda i, j: (i, 0)),
                  pl.BlockSpec((tm, kb), lambda i, j: (i, 0)),
                  pl.BlockSpec((ka, tn), lambda i, j: (0, j)),
                  pl.BlockSpec((kb, tn), lambda i, j: (1, j)),
                  pl.BlockSpec((tm, tn), lambda i, j: (i, j)),
                  pl.BlockSpec((1, 1, tn), lambda i, j: (jnp.minimum(i // per, g - 1), 0, j))],
        out_specs=pl.BlockSpec((tm, tn), lambda i, j: (i, j)),
        out_shape=jax.ShapeDtypeStruct((m, n), F32),
        compiler_params=_cparams("parallel", "arbitrary"),
        name="mm_res2",
    )(a, b, w, w, res, gate)


def _axial_rope_tables(T):
    rows = T // GRID_W
    row, col = jnp.meshgrid(jnp.arange(rows), jnp.arange(GRID_W), indexing='ij')
    inv_freq = ROPE_BASE ** (-jnp.arange(0, AXIS_DIM, 2, dtype=jnp.float32) / AXIS_DIM)

    def angles(pos):
        ang = pos.reshape(-1).astype(jnp.float32)[:, None] * inv_freq
        return jnp.concatenate([ang, ang], axis=-1)

    ang = jnp.concatenate([angles(row), angles(col)], axis=-1)
    return jnp.cos(ang), jnp.sin(ang)


def _moe(hall, groups, gain, shift, scale, gate, rw_t, wg, wu, wd, layer):
    m, d = hall.shape
    ne = rw_t.shape[0]
    set_len0 = groups[0][2]
    hm, aff = _router(hall, gain, shift, scale, rw_t, set_len0, 512)
    idx_parts, gate_parts, meta = [], [], []
    r_off = 0
    for row_off, n_sets, set_len in groups:
        cap = EC_CAPACITY * set_len // ne
        a = aff[:, row_off:row_off + n_sets * set_len].reshape(ne, n_sets, set_len)
        gts, idx = lax.top_k(a, cap)
        rows = idx + (row_off + jnp.arange(n_sets, dtype=jnp.int32) * set_len)[None, :, None]
        idx_parts.append(rows.reshape(ne, n_sets * cap))
        gate_parts.append(gts.reshape(ne, n_sets * cap))
        meta.append((row_off, n_sets, set_len, cap, r_off, idx))
        r_off += n_sets * cap
    rows_per_expert = r_off
    idx_all = jnp.concatenate(idx_parts, axis=1).astype(jnp.int32)
    gates_all = jnp.concatenate(gate_parts, axis=1)
    y = _experts(hm, idx_all.reshape(ne, 1, rows_per_expert), gates_all.reshape(ne, rows_per_expert, 1),
                 wg, wu, wd, layer)
    y2d = y.reshape(ne * rows_per_expert, d)
    e_base = (jnp.arange(ne, dtype=jnp.int32) * rows_per_expert)[None, :, None]
    for gi, (row_off, n_sets, set_len, cap, r0, idx) in enumerate(meta):
        tok = jnp.swapaxes(idx, 0, 1).astype(jnp.int32)
        yrow = e_base + r0 + (jnp.arange(n_sets, dtype=jnp.int32) * cap)[:, None, None] \
            + jnp.arange(cap, dtype=jnp.int32)[None, None, :]
        if gi == 0:
            tabs = _combine_tables(tok.reshape(n_sets, ne * cap), yrow.reshape(n_sets, ne * cap), set_len)
            for s in range(n_sets):
                hall = _combine(hall, y2d, gate[s:s + 1], *(t[s] for t in tabs), row_off + s * set_len, set_len, ne)
        else:
            tok = tok + (jnp.arange(n_sets, dtype=jnp.int32) * set_len)[:, None, None]
            tabs = _combine_tables(tok.reshape(1, -1), yrow.reshape(1, -1), n_sets * set_len)
            hall = _combine(hall, y2d, gate[-1:], *(t[0] for t in tabs), row_off, n_sets * set_len, ne)
    return hall


def kernel(x, c, ctx, c_ctx, mod_w, mod_b, norm_mix, norm_ffn, ab_w_in, gla_gate_w, gla_gate_b, gla_norm,
           sg_norm_w, sg_norm_b, sg_ws, sg_bs, ab_w_out, c_w_in, sinks, c_w_out, router_w, moe_w_gate,
           moe_w_up, moe_w_down, final_norm):
    B, T, D = x.shape
    L = ctx.shape[1]
    ML, MC = B * T, B * L
    cos, sin = _axial_rope_tables(T)

    cond = jnp.concatenate([jax.nn.silu(c), jax.nn.silu(c_ctx)[None], jnp.zeros((3, D), x.dtype)], axis=0)
    hall = jnp.concatenate([x.reshape(ML, D), ctx.reshape(MC, D)], axis=0)

    for layer in range(DEPTH):
        last = layer == DEPTH - 1
        mods = _mm_bias(cond.astype(BF), mod_w, layer, mod_b[layer].reshape(1, N_MOD * D), 512)[:B + 1]
        mods = mods.reshape(B + 1, N_MOD, 1, D)
        mod = lambda i: mods[:, i]

        xall = _norm_mod(hall, norm_mix[layer], mod(0), mod(1), T, 512, BF)

        if layer % 2 == 0:
            e = layer // 2
            w_in = ab_w_in[e]
            w_main = jnp.concatenate([w_in[:, :6144], w_in[:, 6176:]], axis=1).astype(BF)
            w_lr = jnp.pad(w_in[:, 6144:6176], ((0, 0), (0, 96))).astype(BF)
            p = _mm(xall, w_main, 1024, 1024)
            lr = _mm(xall, w_lr, 1024, LANES)
            gw = [jnp.zeros((LANES, GLA_K_W), F32).at[d * GLA_RANK:(d + 1) * GLA_RANK].set(gla_gate_w[e, d]).astype(BF)
                  for d in range(2)]
            o_fwd = _gla(p, lr, gw[0], gla_gate_b[e, 0:1], None, None, B, T, L, reverse=False)
            a_mix = _gla(p, lr, gw[1], gla_gate_b[e, 1:2], o_fwd, gla_norm[e].reshape(1, GLA_V_W), B, T, L,
                         reverse=True)
            b_mix = _spatial_gate(p, sg_norm_w[e].reshape(1, SG_W), sg_norm_b[e].reshape(1, SG_W), sg_ws[e],
                                  sg_bs[e].reshape(SG_GROUPS, SG_CHUNK, 1))
            hall = _mm_res2(a_mix, b_mix, ab_w_out[e].astype(BF), hall, mod(2), T, 1024, 1024)
        else:
            o = layer // 2
            p = _mm(xall, c_w_in[o].astype(BF), 1024, 1024)
            att = _attention(p, cos, sin, sinks[o], B, T, L)
            assert last, "an attention layer that is not the last would also have to update the context stream"
            hall = _mm_res(att, c_w_out[o].astype(BF), hall, mods[:B, 2], T, 1024, 1024)

        rw_t = router_w[layer].T.astype(BF)
        moe_w = (moe_w_gate, moe_w_up, moe_w_down, layer)
        if last:
            hall = _moe(hall, [(0, B, T)], norm_ffn[layer], mods[:B, 3], mods[:B, 4], mods[:B, 5], rw_t, *moe_w)
        else:
            hall = _moe(hall, [(0, B, T), (ML, B, L)], norm_ffn[layer], mod(3), mod(4), mod(5), rw_t, *moe_w)

    zero = jnp.zeros((1, 1, D), jnp.float32)
    out = _norm_mod(hall[:ML], final_norm, zero, zero, ML, 512, jnp.float32)
    return out.reshape(B, T, D)
```

```python
import functools

import jax
import jax.numpy as jnp
import numpy as np
from jax import lax
from jax.experimental import pallas as pl
from jax.experimental.pallas import tpu as pltpu

D_MODEL = 4096
DEPTH = 2
GRID_W = 64
EPS = 1e-6
NEG_INF = -1e30
N_MOD = 6
MIX_W = D_MODEL
GLA_HEADS = 4
GLA_V_W = MIX_W // 2
GLA_DV = GLA_V_W // GLA_HEADS
GLA_K_W = GLA_V_W // 2
GLA_DK = GLA_K_W // GLA_HEADS
GLA_RANK = 16
GLA_TAU = 16.0
GLA_CHUNK = 64
SG_GROUPS = 4
SG_W = MIX_W // 2
SG_GC = SG_W // SG_GROUPS
SG_CHUNK = 128
ATT_HEADS = D_MODEL // 128
KV_HEADS = 8
HEAD_DIM = 128
GQ = ATT_HEADS // KV_HEADS
WINDOW = 128
ATT_BLOCK = 128
AXIS_DIM = HEAD_DIM // 2
ROPE_BASE = 10000.0
Q_W = ATT_HEADS * HEAD_DIM
KV_W = KV_HEADS * HEAD_DIM
N_EXPERTS = 16
EC_CAPACITY = 2

V7X_VMEM_LIMIT_BYTES = 56 * 1024 * 1024
LANES = 128

BF = jnp.bfloat16
F32 = jnp.float32


def _cparams(*sem, vmem_limit_bytes=V7X_VMEM_LIMIT_BYTES):
    return pltpu.CompilerParams(dimension_semantics=sem, vmem_limit_bytes=vmem_limit_bytes)


def _mm_body(x_ref, w_ref, o_ref):
    o_ref[...] = jnp.dot(x_ref[...], w_ref[...], preferred_element_type=F32).astype(o_ref.dtype)


def _mm(x, w, tm, tn, out_dtype=F32):
    m, k = x.shape
    n = w.shape[1]
    assert m % tm == 0 and n % tn == 0, (m, n, tm, tn)
    return pl.pallas_call(
        _mm_body,
        grid=(m // tm, n // tn),
        in_specs=[pl.BlockSpec((tm, k), lambda i, j: (i, 0)),
                  pl.BlockSpec((k, tn), lambda i, j: (0, j))],
        out_specs=pl.BlockSpec((tm, tn), lambda i, j: (i, j)),
        out_shape=jax.ShapeDtypeStruct((m, n), out_dtype),
        compiler_params=_cparams("parallel", "arbitrary"),
        name="mm",
    )(x, w)


def _mm_bias_body(x_ref, w_ref, b_ref, o_ref):
    o_ref[...] = jnp.dot(x_ref[...], w_ref[...].astype(BF), preferred_element_type=F32) + b_ref[...]


def _mm_bias(x, w, layer, bias, tn):
    m, k = x.shape
    n = w.shape[2]
    assert n % tn == 0
    return pl.pallas_call(
        _mm_bias_body,
        grid=(n // tn,),
        in_specs=[pl.BlockSpec((m, k), lambda j: (0, 0)),
                  pl.BlockSpec((None, k, tn), lambda j: (layer, 0, j)),
                  pl.BlockSpec((1, tn), lambda j: (0, j))],
        out_specs=pl.BlockSpec((m, tn), lambda j: (0, j)),
        out_shape=jax.ShapeDtypeStruct((m, n), F32),
        compiler_params=_cparams("parallel"),
        name="mm_bias",
    )(x, w, bias)


def _mm_res_body(x_ref, w_ref, r_ref, g_ref, o_ref):
    acc = jnp.dot(x_ref[...], w_ref[...], preferred_element_type=F32)
    o_ref[...] = r_ref[...] + g_ref[0] * acc


def _mm_res(x, w, res, gate, rows_per_group, tm, tn):
    m, k = x.shape
    n = w.shape[1]
    g = gate.shape[0]
    per = rows_per_group // tm
    assert m % tm == 0 and n % tn == 0 and rows_per_group % tm == 0
    return pl.pallas_call(
        _mm_res_body,
        grid=(m // tm, n // tn),
        in_specs=[pl.BlockSpec((tm, k), lambda i, j: (i, 0)),
                  pl.BlockSpec((k, tn), lambda i, j: (0, j)),
                  pl.BlockSpec((tm, tn), lambda i, j: (i, j)),
                  pl.BlockSpec((1, 1, tn), lambda i, j: (jnp.minimum(i // per, g - 1), 0, j))],
        out_specs=pl.BlockSpec((tm, tn), lambda i, j: (i, j)),
        out_shape=jax.ShapeDtypeStruct((m, n), F32),
        compiler_params=_cparams("parallel", "arbitrary"),
        name="mm_res",
    )(x, w, res, gate)


def _modulated(x, gain, shift, scale):
    y = x * lax.rsqrt(jnp.mean(x * x, axis=-1, keepdims=True) + EPS)
    return (y * gain) * (1.0 + scale) + shift


def _norm_mod_body(x_ref, g_ref, sh_ref, sc_ref, o_ref):
    o_ref[...] = _modulated(x_ref[...], g_ref[...], sh_ref[0], sc_ref[0]).astype(o_ref.dtype)


def _group_index(rows_per_group, tm, groups):
    per = rows_per_group // tm
    assert rows_per_group % tm == 0
    return lambda i: (jnp.minimum(i // per, groups - 1), 0, 0)


def _norm_mod(h, gain, shift, scale, rows_per_group, tm, out_dtype):
    m, d = h.shape
    grp = _group_index(rows_per_group, tm, shift.shape[0])
    return pl.pallas_call(
        _norm_mod_body,
        grid=(m // tm,),
        in_specs=[pl.BlockSpec((tm, d), lambda i: (i, 0)),
                  pl.BlockSpec((1, d), lambda i: (0, 0)),
                  pl.BlockSpec((1, 1, d), grp),
                  pl.BlockSpec((1, 1, d), grp)],
        out_specs=pl.BlockSpec((tm, d), lambda i: (i, 0)),
        out_shape=jax.ShapeDtypeStruct((m, d), out_dtype),
        compiler_params=_cparams("parallel"),
        name="norm_mod",
    )(h, gain.reshape(1, d), shift, scale)


def _router_body(x_ref, g_ref, sh_ref, sc_ref, rwt_ref, hm_ref, aff_ref):
    hm = _modulated(x_ref[...], g_ref[...], sh_ref[0], sc_ref[0])
    hm_ref[...] = hm
    logits = lax.dot_general(rwt_ref[...], hm.astype(BF), (((1,), (1,)), ((), ())),
                             preferred_element_type=F32)
    e = jnp.exp(logits - jnp.max(logits, axis=0, keepdims=True))
    aff_ref[...] = e / jnp.sum(e, axis=0, keepdims=True)


def _router(h, gain, shift, scale, rw_t, rows_per_group, tm):
    m, d = h.shape
    ne = rw_t.shape[0]
    grp = _group_index(rows_per_group, tm, shift.shape[0])
    return pl.pallas_call(
        _router_body,
        grid=(m // tm,),
        in_specs=[pl.BlockSpec((tm, d), lambda i: (i, 0)),
                  pl.BlockSpec((1, d), lambda i: (0, 0)),
                  pl.BlockSpec((1, 1, d), grp),
                  pl.BlockSpec((1, 1, d), grp),
                  pl.BlockSpec((ne, d), lambda i: (0, 0))],
        out_specs=[pl.BlockSpec((tm, d), lambda i: (i, 0)),
                   pl.BlockSpec((ne, tm), lambda i: (0, i))],
        out_shape=[jax.ShapeDtypeStruct((m, d), F32), jax.ShapeDtypeStruct((ne, m), F32)],
        compiler_params=_cparams("parallel"),
        name="router",
    )(h, gain.reshape(1, d), shift, scale, rw_t)


GATHER_ROWS = 128
GATHER_BUFS = 3
EXPERT_K_SPLIT = 2
EXPERTS_VMEM_LIMIT_BYTES = 60 * 1024 * 1024


def _experts_body(idx_vm, idx_next_vm, gates_ref, hm_hbm, wg_ref, wu_ref, wd_ref, y_ref,
                  idx_sm, xg, hid, g_acc, u_acc, stage, sem, *, rows, mm_rows, n_ff_tiles, n_d_tiles):
    e = pl.program_id(0)
    j = pl.program_id(1)
    n_experts = pl.num_programs(0)
    ch, nb, ks = GATHER_ROWS, GATHER_BUFS, EXPERT_K_SPLIT
    n_chunks = rows // ch
    tf = hid.shape[2]
    kd = xg.shape[2]
    n_up, n_down = n_ff_tiles * ks, n_d_tiles * ks

    def issue(q):
        def body(r, carry):
            src = idx_sm[0, q * ch + r]
            pltpu.make_async_copy(hm_hbm.at[pl.ds(src, 1)], stage.at[q % nb, pl.ds(r, 1)], sem.at[q % nb]).start()
            return carry
        lax.fori_loop(0, ch, body, 0, unroll=8)

    def finish(q):
        pltpu.make_async_copy(hm_hbm.at[pl.ds(0, ch)], stage.at[q % nb], sem.at[q % nb]).wait()
        for kh in range(ks):
            xg[kh, q * ch:(q + 1) * ch, :] = stage[q % nb, :, kh * kd:(kh + 1) * kd].astype(BF)

    def gather(lo, hi, first):
        if first:
            for q in range(min(nb, n_chunks)):
                issue(q)
        for q in range(lo, hi):
            finish(q)
            if q + nb < n_chunks:
                issue(q + nb)

    @pl.when((e == 0) & (j == 0))
    def _first_gather():
        pltpu.sync_copy(idx_vm.at[0], idx_sm)
        gather(0, n_chunks, True)

    for k in range(n_down):
        lo, hi = k * n_chunks // n_down, (k + 1) * n_chunks // n_down
        if k == 0 or hi > lo:
            @pl.when((j == n_up + k) & (e + 1 < n_experts))
            def _next_gather(k=k, lo=lo, hi=hi):
                if k == 0:
                    pltpu.sync_copy(idx_next_vm.at[0], idx_sm)
                gather(lo, hi, k == 0)

    @pl.when(j < n_up)
    def _up():
        f, kh = j // ks, j % ks
        wg = wg_ref[0, 0, 0].astype(BF)
        wu = wu_ref[0, 0, 0].astype(BF)
        for rc in range(rows // mm_rows):
            rs = slice(rc * mm_rows, (rc + 1) * mm_rows)
            xs = xg[kh, rs, :]
            g = jnp.dot(xs, wg, preferred_element_type=F32)
            u = jnp.dot(xs, wu, preferred_element_type=F32)

            @pl.when(kh == 0)
            def _start():
                g_acc[rs, :] = g
                u_acc[rs, :] = u

            @pl.when((kh > 0) & (kh < ks - 1))
            def _accumulate():
                g_acc[rs, :] += g
                u_acc[rs, :] += u

            @pl.when(kh == ks - 1)
            def _finish():
                gt = g_acc[rs, :] + g
                ut = u_acc[rs, :] + u
                hid[f, rs, :] = ((gt * jax.nn.sigmoid(gt)) * ut).astype(BF)

    @pl.when(j >= n_up)
    def _down():
        kh = (j - n_up) % ks
        per = n_ff_tiles // ks
        wd = wd_ref[0, 0, 0].astype(BF)
        for rc in range(rows // mm_rows):
            rs = slice(rc * mm_rows, (rc + 1) * mm_rows)
            acc = jnp.dot(hid[kh * per, rs, :], wd[0:tf, :], preferred_element_type=F32)
            for f in range(1, per):
                acc = acc + jnp.dot(hid[kh * per + f, rs, :], wd[f * tf:(f + 1) * tf, :], preferred_element_type=F32)
            acc = acc * gates_ref[0, rs, :]

            @pl.when(kh == 0)
            def _start():
                y_ref[0, rs, :] = acc

            @pl.when(kh > 0)
            def _accumulate():
                y_ref[0, rs, :] += acc


def _experts(hm, idx, gates, wg, wu, wd, layer, tf=256, td=512):
    ks = EXPERT_K_SPLIT
    ne, _, rows = idx.shape
    d = hm.shape[1]
    ff = wg.shape[3]
    n_ff_tiles, n_d_tiles = ff // tf, d // td
    n_up, n_down = n_ff_tiles * ks, n_d_tiles * ks
    mm_rows = rows
    assert rows % GATHER_ROWS == 0 and mm_rows % 8 == 0 and n_ff_tiles % ks == 0 and d % ks == 0
    body = functools.partial(_experts_body, rows=rows, mm_rows=mm_rows, n_ff_tiles=n_ff_tiles, n_d_tiles=n_d_tiles)
    up = lambda j: jnp.minimum(j, n_up - 1)
    down = lambda j: jnp.maximum(j - n_up, 0)
    wg4 = wg.reshape(wg.shape[0], ne, ks, d // ks, ff)
    wu4 = wu.reshape(wu.shape[0], ne, ks, d // ks, ff)
    wd4 = wd.reshape(wd.shape[0], ne, ks, ff // ks, d)
    up_spec = pl.BlockSpec((1, 1, 1, d // ks, tf), lambda e, j: (layer, e, up(j) % ks, 0, up(j) // ks))
    return pl.pallas_call(
        body,
        grid=(ne, n_up + n_down),
        in_specs=[pl.BlockSpec((1, 1, rows), lambda e, j: (e, 0, 0)),
                  pl.BlockSpec((1, 1, rows), lambda e, j: (jnp.minimum(e + 1, ne - 1), 0, 0)),
                  pl.BlockSpec((1, rows, 1), lambda e, j: (e, 0, 0)),
                  pl.BlockSpec(memory_space=pl.ANY),
                  up_spec,
                  up_spec,
                  pl.BlockSpec((1, 1, 1, ff // ks, td), lambda e, j: (layer, e, down(j) % ks, 0, down(j) // ks))],
        out_specs=pl.BlockSpec((1, rows, td), lambda e, j: (e, 0, down(j) // ks)),
        out_shape=jax.ShapeDtypeStruct((ne, rows, d), F32),
        scratch_shapes=[pltpu.SMEM((1, rows), jnp.int32),
                        pltpu.VMEM((ks, rows, d // ks), BF),
                        pltpu.VMEM((n_ff_tiles, rows, tf), BF),
                        pltpu.VMEM((rows, tf), F32),
                        pltpu.VMEM((rows, tf), F32),
                        pltpu.VMEM((GATHER_BUFS, GATHER_ROWS, d), F32),
                        pltpu.SemaphoreType.DMA((GATHER_BUFS,))],
        compiler_params=_cparams("arbitrary", "arbitrary", vmem_limit_bytes=EXPERTS_VMEM_LIMIT_BYTES),
        name="experts",
    )(idx, idx, gates, hm, wg4, wu4, wd4)


COMBINE_TOKENS = 32
COMBINE_BUFS = 4
COMBINE_STRIP = 512
COMBINE_UNROLL = 4


def _combine_body(pk_ref, ts_ref, mc_ref, h_ref, cnt_ref, gate_ref, y_hbm, o_ref, buf, sem, *, n_tiles, n_slots):
    g = pl.program_id(0)
    tt, d = h_ref.shape
    nb = buf.shape[0]

    def issue(t):
        slot = t % nb

        def start(p):
            v = pk_ref[p]
            src = jnp.bitwise_and(v, 0xFFFF)
            dst = jnp.right_shift(v, 16)
            pltpu.make_async_copy(y_hbm.at[pl.ds(src, 1)], buf.at[slot, pl.ds(dst, 1)], sem.at[slot]).start()

        lo, hi = ts_ref[t], ts_ref[t + 1]
        n_groups = (hi - lo) // COMBINE_UNROLL

        def group(i, carry):
            for u in range(COMBINE_UNROLL):
                start(lo + i * COMBINE_UNROLL + u)
            return carry
        lax.fori_loop(0, n_groups, group, 0)

        def single(p, carry):
            start(p)
            return carry
        lax.fori_loop(lo + n_groups * COMBINE_UNROLL, hi, single, 0)

    def drain(t):
        slot = t % nb
        n = ts_ref[t + 1] - ts_ref[t]
        for bit in range((n_slots * tt).bit_length()):
            @pl.when(jnp.bitwise_and(n, 1 << bit) != 0)
            def _wait(bit=bit):
                rows = pl.ds(0, 1 << bit)
                pltpu.make_async_copy(y_hbm.at[rows], buf.at[slot, rows], sem.at[slot]).wait()

    @pl.when(g == 0)
    def _first():
        def zero(r, carry):
            buf[:, pl.ds(pl.multiple_of(r * tt, tt), tt), :] = jnp.zeros((nb, tt, d), F32)
            return carry
        lax.fori_loop(0, n_slots, zero, 0)
        for t in range(min(nb - 1, n_tiles)):
            issue(t)

    @pl.when(g + nb - 1 < n_tiles)
    def _prefetch():
        issue(g + nb - 1)

    slot = g % nb
    drain(g)
    n_used = mc_ref[g]
    cnt = jnp.broadcast_to(cnt_ref[...], (tt, COMBINE_STRIP))
    for cs in range(d // COMBINE_STRIP):
        cols = slice(cs * COMBINE_STRIP, (cs + 1) * COMBINE_STRIP)

        def body(k, acc):
            v = buf[slot, pl.ds(pl.multiple_of(k * tt, tt), tt), cols]
            return acc + jnp.where(cnt > k, v, 0.0)

        acc = lax.fori_loop(0, n_used, body, jnp.zeros((tt, COMBINE_STRIP), F32))
        o_ref[:, cols] = h_ref[:, cols] + gate_ref[0][:, cols] * acc


def _combine_tables(tok, yrow, n_tok):
    tt = COMBINE_TOKENS
    s, n_pairs = tok.shape
    key = jnp.sort(jnp.bitwise_or(jnp.left_shift(tok, 16), yrow), axis=-1)
    tok_s = jnp.right_shift(key, 16)
    src = jnp.bitwise_and(key, 0xFFFF)
    cnt = jnp.sum(tok[:, :, None] == jnp.arange(n_tok, dtype=jnp.int32)[None, None, :], axis=1, dtype=jnp.int32)
    start = jnp.concatenate([jnp.zeros((s, 1), jnp.int32), jnp.cumsum(cnt, axis=1, dtype=jnp.int32)], axis=1)
    pos = jnp.broadcast_to(jnp.arange(n_pairs, dtype=jnp.int32), (s, n_pairs))
    is_first = jnp.concatenate([jnp.ones((s, 1), bool), tok_s[:, 1:] != tok_s[:, :-1]], axis=1)
    k = pos - lax.cummax(jnp.where(is_first, pos, 0), axis=1)
    packed = jnp.bitwise_or(src, jnp.left_shift(k * tt + tok_s % tt, 16))
    return packed, start[:, ::tt], jnp.max(cnt.reshape(s, n_tok // tt, tt), axis=2), cnt


def _combine(h, y2d, gate, packed, tile_start, tile_slots, cnt, row_off, rows_per_group, n_slots):
    tt = COMBINE_TOKENS
    d = h.shape[1]
    n_tok = cnt.shape[0]
    n_tiles = n_tok // tt
    assert n_tok % tt == 0 and row_off % tt == 0 and y2d.shape[0] <= 0x10000 and n_slots * tt <= 0x7FFF
    tile0 = row_off // tt
    per = rows_per_group // tt
    n_groups = gate.shape[0]
    grid_spec = pltpu.PrefetchScalarGridSpec(
        num_scalar_prefetch=3,
        grid=(n_tiles,),
        in_specs=[pl.BlockSpec((tt, d), lambda g, *_: (tile0 + g, 0)),
                  pl.BlockSpec((tt, 1), lambda g, *_: (g, 0)),
                  pl.BlockSpec((1, 1, d), lambda g, *_: (jnp.minimum(g // per, n_groups - 1), 0, 0)),
                  pl.BlockSpec(memory_space=pl.ANY)],
        out_specs=pl.BlockSpec((tt, d), lambda g, *_: (tile0 + g, 0)),
        scratch_shapes=[pltpu.VMEM((COMBINE_BUFS, n_slots * tt, d), F32),
                        pltpu.SemaphoreType.DMA((COMBINE_BUFS,))],
    )
    return pl.pallas_call(
        functools.partial(_combine_body, n_tiles=n_tiles, n_slots=n_slots),
        grid_spec=grid_spec,
        out_shape=jax.ShapeDtypeStruct(h.shape, F32),
        input_output_aliases={3: 0},
        compiler_params=_cparams("arbitrary"),
        name="combine",
    )(packed, tile_start, tile_slots, h, cnt.reshape(n_tok, 1), gate, y2d)


ATT_Q_TILE = 512


def _rope(x, cos, sin):
    lane = lax.broadcasted_iota(jnp.int32, x.shape, 1)
    fwd = pltpu.roll(x, AXIS_DIM // 2, axis=1)
    bwd = pltpu.roll(x, HEAD_DIM - AXIS_DIM // 2, axis=1)
    rot = jnp.where(lane % AXIS_DIM < AXIS_DIM // 2, -bwd, fwd)
    return x * cos + rot * sin


def _attn_body(q_ref, kp_ref, km_ref, kn_ref, vp_ref, vm_ref, vn_ref, kc_ref, vc_ref,
               cm_ref, sm_ref, cp_ref, sp_ref, cn_ref, sn_ref, sink_ref, o_ref, *, n_q_tiles):
    i = pl.program_id(2)
    qt = q_ref.shape[0]
    blk = ATT_BLOCK
    nb = qt // blk
    scale = HEAD_DIM ** -0.5
    cos_w = jnp.concatenate([cp_ref[...], cm_ref[...], cn_ref[...]], axis=0)
    sin_w = jnp.concatenate([sp_ref[...], sm_ref[...], sn_ref[...]], axis=0)
    k_w = jnp.concatenate([kp_ref[...], km_ref[...], kn_ref[...]], axis=0)
    k_w = _rope(k_w, cos_w, sin_w).astype(BF)
    v_w = jnp.concatenate([vp_ref[...], vm_ref[...], vn_ref[...]], axis=0).astype(BF)
    kc = kc_ref[...].astype(BF)
    vc = vc_ref[...].astype(BF)
    sink = sink_ref[0]
    nt = (((1,), (1,)), ((), ()))
    band = 3 * blk
    row = lax.broadcasted_iota(jnp.int32, (GQ * blk, band), 0) % blk
    col = lax.broadcasted_iota(jnp.int32, (GQ * blk, band), 1)
    in_window = (col >= row) & (col <= row + 2 * WINDOW)
    for jb in range(nb):
        rs = slice(jb * blk, (jb + 1) * blk)
        cos_q = cm_ref[rs, :]
        sin_q = sm_ref[rs, :]
        qs = jnp.concatenate([_rope(q_ref[rs, g * HEAD_DIM:(g + 1) * HEAD_DIM], cos_q, sin_q)
                              for g in range(GQ)], axis=0).astype(BF)
        kb = k_w[jb * blk:jb * blk + band]
        vb = v_w[jb * blk:jb * blk + band]
        s_loc = lax.dot_general(qs, kb, nt, preferred_element_type=F32) * scale
        s_ctx = lax.dot_general(qs, kc, nt, preferred_element_type=F32) * scale
        valid = in_window
        if jb == 0:
            valid = valid & (col >= jnp.where(i == 0, blk, 0))
        if jb == nb - 1:
            valid = valid & (col < jnp.where(i == n_q_tiles - 1, 2 * blk, band))
        s_loc = jnp.where(valid, s_loc, NEG_INF)
        m = jnp.maximum(jnp.maximum(jnp.max(s_loc, axis=1, keepdims=True),
                                    jnp.max(s_ctx, axis=1, keepdims=True)), sink)
        e_loc = jnp.exp(s_loc - m)
        e_ctx = jnp.exp(s_ctx - m)
        den = (jnp.sum(e_loc, axis=1, keepdims=True) + jnp.sum(e_ctx, axis=1, keepdims=True)
               + jnp.exp(sink - m))
        o = (jnp.dot(e_loc.astype(BF), vb, preferred_element_type=F32)
             + jnp.dot(e_ctx.astype(BF), vc, preferred_element_type=F32)) / den
        for g in range(GQ):
            o_ref[rs, g * HEAD_DIM:(g + 1) * HEAD_DIM] = o[g * blk:(g + 1) * blk].astype(o_ref.dtype)


def _attention(p, cos, sin, sinks, n_batch, seq, ctx_len):
    qt, blk = ATT_Q_TILE, ATT_BLOCK
    ml = n_batch * seq
    nq = seq // qt
    nbk = seq // blk
    bpt = qt // blk
    kcol, vcol = Q_W // HEAD_DIM, (Q_W + KV_W) // HEAD_DIM
    assert seq % qt == 0 and ml % ctx_len == 0
    sink_col = jnp.repeat(sinks.astype(F32).reshape(KV_HEADS, GQ, 1), blk, axis=2).reshape(KV_HEADS, GQ * blk, 1)
    prev_blk = lambda i: jnp.maximum(i * bpt - 1, 0)
    next_blk = lambda i: jnp.minimum((i + 1) * bpt, nbk - 1)
    qspec = pl.BlockSpec((qt, GQ * HEAD_DIM), lambda b, h, i: (b * nq + i, h))

    def kv_specs(c0):
        return [pl.BlockSpec((blk, HEAD_DIM), lambda b, h, i: (b * nbk + prev_blk(i), c0 + h)),
                pl.BlockSpec((qt, HEAD_DIM), lambda b, h, i: (b * nq + i, c0 + h)),
                pl.BlockSpec((blk, HEAD_DIM), lambda b, h, i: (b * nbk + next_blk(i), c0 + h))]

    ctx_spec = lambda c0: pl.BlockSpec((ctx_len, HEAD_DIM), lambda b, h, i: (ml // ctx_len + b, c0 + h))
    tab_specs = [pl.BlockSpec((qt, HEAD_DIM), lambda b, h, i: (i, 0)),
                 pl.BlockSpec((qt, HEAD_DIM), lambda b, h, i: (i, 0)),
                 pl.BlockSpec((blk, HEAD_DIM), lambda b, h, i: (prev_blk(i), 0)),
                 pl.BlockSpec((blk, HEAD_DIM), lambda b, h, i: (prev_blk(i), 0)),
                 pl.BlockSpec((blk, HEAD_DIM), lambda b, h, i: (next_blk(i), 0)),
                 pl.BlockSpec((blk, HEAD_DIM), lambda b, h, i: (next_blk(i), 0))]
    return pl.pallas_call(
        functools.partial(_attn_body, n_q_tiles=nq),
        grid=(n_batch, KV_HEADS, nq),
        in_specs=[qspec] + kv_specs(kcol) + kv_specs(vcol) + [ctx_spec(kcol), ctx_spec(vcol)] + tab_specs
                 + [pl.BlockSpec((1, GQ * blk, 1), lambda b, h, i: (h, 0, 0))],
        out_specs=pl.BlockSpec((qt, GQ * HEAD_DIM), lambda b, h, i: (b * nq + i, h)),
        out_shape=jax.ShapeDtypeStruct((ml, Q_W), BF),
        compiler_params=_cparams("parallel", "parallel", "arbitrary"),
        name="attention",
    )(p, p, p, p, p, p, p, p, p, cos, sin, cos, sin, cos, sin, sink_col)


GLA_TOKENS = 256
_NT = (((1,), (1,)), ((), ()))
_TN = (((0,), (0,)), ((), ()))


def _gla_body(*refs, reverse, fuse_output):
    if fuse_output:
        q_ref, k_ref, v_ref, lr_ref, gw_ref, gb_ref, of_ref, g_ref, gain_ref, o_ref, st_ref = refs
    else:
        q_ref, k_ref, v_ref, lr_ref, gw_ref, gb_ref, o_ref, st_ref = refs
    tb = q_ref.shape[0]
    ch = GLA_CHUNK
    n_ch = tb // ch

    @pl.when(pl.program_id(1) == 0)
    def _reset():
        st_ref[...] = jnp.zeros(st_ref.shape, F32)

    r = lax.broadcasted_iota(jnp.int32, (tb, tb), 0)
    c = lax.broadcasted_iota(jnp.int32, (tb, tb), 1)
    same = (r // ch) == (c // ch)
    tri = same & ((c >= r) if reverse else (c <= r))
    tri_bf = jnp.where(tri, 1.0, 0.0).astype(BF)
    same_bf = jnp.where(same, 1.0, 0.0).astype(BF)

    z = jnp.dot(lr_ref[...].astype(BF), gw_ref[...], preferred_element_type=F32) + gb_ref[...]
    la = jax.nn.log_sigmoid(z) * (1.0 / GLA_TAU)
    la_hi = la.astype(BF)
    la_lo = (la - la_hi.astype(F32)).astype(BF)
    b = (jnp.dot(tri_bf, la_hi, preferred_element_type=F32)
         + jnp.dot(tri_bf, la_lo, preferred_element_type=F32))
    tot = (jnp.dot(same_bf, la_hi, preferred_element_type=F32)
           + jnp.dot(same_bf, la_lo, preferred_element_type=F32))
    q = q_ref[...] * (GLA_DK ** -0.5)
    k = k_ref[...]
    v = v_ref[...].astype(BF)
    q_in = (q * jnp.exp(b)).astype(BF)
    k_in = (k * jnp.exp(-b)).astype(BF)
    k_dec = (k * jnp.exp(tot - b)).astype(BF)
    dec = jnp.exp(tot)
    for h in range(GLA_HEADS):
        ks = slice(h * GLA_DK, (h + 1) * GLA_DK)
        vs = slice(h * GLA_DV, (h + 1) * GLA_DV)
        sc = lax.dot_general(q_in[:, ks], k_in[:, ks], _NT, preferred_element_type=F32)
        sc = jnp.where(tri, sc, 0.0).astype(BF)
        o_intra = jnp.dot(sc, v[:, vs], preferred_element_type=F32)
        outs = [None] * n_ch
        for ci in (range(n_ch - 1, -1, -1) if reverse else range(n_ch)):
            rs = slice(ci * ch, (ci + 1) * ch)
            st = st_ref[h]
            outs[ci] = o_intra[rs] + lax.dot_general(q_in[rs, ks], st.astype(BF), _NT, preferred_element_type=F32)
            upd = lax.dot_general(v[rs, vs], k_dec[rs, ks], _TN, preferred_element_type=F32)
            st_ref[h] = st * dec[ci * ch:ci * ch + 1, ks] + upd
        o = jnp.concatenate(outs, axis=0)
        if fuse_output:
            o = o + of_ref[:, vs]
            y = o * lax.rsqrt(jnp.mean(o * o, axis=-1, keepdims=True) + EPS) * gain_ref[:, vs]
            g = g_ref[:, vs]
            o_ref[:, vs] = (y * (g * jax.nn.sigmoid(g))).astype(o_ref.dtype)
        else:
            o_ref[:, vs] = o


def _gla(p, lr, gate_w, gate_b, o_fwd, gain, n_batch, seq, ctx_len, *, reverse):
    tb = GLA_TOKENS
    ml = n_batch * seq
    m = p.shape[0]
    nc, nl = ctx_len // tb, seq // tb
    assert ctx_len % tb == 0 and seq % tb == 0 and ml % tb == 0
    ctx_base = ml // tb

    def row(b, s):
        sc_ = (nc - 1 - s) if reverse else s
        sl_ = (nl - 1 - (s - nc)) if reverse else (s - nc)
        return jnp.where(s < nc, ctx_base + b * nc + sc_, b * nl + sl_)

    kk, kv, kg = 1, 2 * GLA_K_W // GLA_V_W, (2 * GLA_K_W + GLA_V_W) // GLA_V_W
    in_specs = [pl.BlockSpec((tb, GLA_K_W), lambda b, s: (row(b, s), 0)),
                pl.BlockSpec((tb, GLA_K_W), lambda b, s: (row(b, s), kk)),
                pl.BlockSpec((tb, GLA_V_W), lambda b, s: (row(b, s), kv)),
                pl.BlockSpec((tb, LANES), lambda b, s: (row(b, s), 0)),
                pl.BlockSpec((LANES, GLA_K_W), lambda b, s: (0, 0)),
                pl.BlockSpec((1, GLA_K_W), lambda b, s: (0, 0))]
    args = [p, p, p, lr, gate_w, gate_b]
    if reverse:
        in_specs += [pl.BlockSpec((tb, GLA_V_W), lambda b, s: (row(b, s), 0)),
                     pl.BlockSpec((tb, GLA_V_W), lambda b, s: (row(b, s), kg)),
                     pl.BlockSpec((1, GLA_V_W), lambda b, s: (0, 0))]
        args += [o_fwd, p, gain]
    return pl.pallas_call(
        functools.partial(_gla_body, reverse=reverse, fuse_output=reverse),
        grid=(n_batch, nc + nl),
        in_specs=in_specs,
        out_specs=pl.BlockSpec((tb, GLA_V_W), lambda b, s: (row(b, s), 0)),
        out_shape=jax.ShapeDtypeStruct((m, GLA_V_W), BF if reverse else F32),
        scratch_shapes=[pltpu.VMEM((GLA_HEADS, GLA_DV, GLA_DK), F32)],
        compiler_params=_cparams("parallel", "arbitrary"),
        name="gla_bwd" if reverse else "gla_fwd",
    )(*args)


SG_TOKENS = 512


def _sg_body(u_ref, vs_ref, nw_ref, nb_ref, ws_ref, bs_ref, o_ref):
    u = jax.nn.gelu(u_ref[...])
    vf = jax.nn.gelu(vs_ref[...])
    mu = jnp.mean(vf, axis=-1, keepdims=True)
    var = jnp.mean(jnp.square(vf - mu), axis=-1, keepdims=True)
    vn = ((vf - mu) * lax.rsqrt(var + EPS) * nw_ref[...] + nb_ref[...]).astype(BF)
    ws = ws_ref[0].astype(BF)
    bias = bs_ref[0]
    for ci in range(u.shape[0] // SG_CHUNK):
        rs = slice(ci * SG_CHUNK, (ci + 1) * SG_CHUNK)
        s = jnp.dot(ws, vn[rs], preferred_element_type=F32) + bias
        o_ref[rs, :] = (u[rs] * s).astype(o_ref.dtype)


def _spatial_gate(p, nw, nb, ws, bias_col):
    m = p.shape[0]
    tt = SG_TOKENS
    ku = (2 * GLA_K_W + 2 * GLA_V_W) // SG_GC
    kv = ku + SG_W // SG_GC
    return pl.pallas_call(
        _sg_body,
        grid=(m // tt, SG_GROUPS),
        in_specs=[pl.BlockSpec((tt, SG_GC), lambda i, g: (i, ku + g)),
                  pl.BlockSpec((tt, SG_GC), lambda i, g: (i, kv + g)),
                  pl.BlockSpec((1, SG_GC), lambda i, g: (0, g)),
                  pl.BlockSpec((1, SG_GC), lambda i, g: (0, g)),
                  pl.BlockSpec((1, SG_CHUNK, SG_CHUNK), lambda i, g: (g, 0, 0)),
                  pl.BlockSpec((1, SG_CHUNK, 1), lambda i, g: (g, 0, 0))],
        out_specs=pl.BlockSpec((tt, SG_GC), lambda i, g: (i, g)),
        out_shape=jax.ShapeDtypeStruct((m, SG_W), BF),
        compiler_params=_cparams("parallel", "arbitrary"),
        name="spatial_gate",
    )(p, p, nw, nb, ws, bias_col)


def _mm_res2_body(a_ref, b_ref, wa_ref, wb_ref, r_ref, g_ref, o_ref):
    acc = (jnp.dot(a_ref[...], wa_ref[...], preferred_element_type=F32)
           + jnp.dot(b_ref[...], wb_ref[...], preferred_element_type=F32))
    o_ref[...] = r_ref[...] + g_ref[0] * acc


def _mm_res2(a, b, w, res, gate, rows_per_group, tm, tn):
    m, ka = a.shape
    kb = b.shape[1]
    n = w.shape[1]
    g = gate.shape[0]
    per = rows_per_group // tm
    assert ka == kb and w.shape[0] == ka + kb and m % tm == 0 and n % tn == 0 and rows_per_group % tm == 0
    return pl.pallas_call(
        _mm_res2_body,
        grid=(m // tm, n // tn),
        in_specs=[pl.BlockSpec((tm, ka), lambda i, j: (i, 0)),
                  pl.BlockSpec((tm, kb), lambda i, j: (i, 0)),
                  pl.BlockSpec((ka, tn), lambda i, j: (0, j)),
                  pl.BlockSpec((kb, tn), lambda i, j: (1, j)),
                  pl.BlockSpec((tm, tn), lambda i, j: (i, j)),
                  pl.BlockSpec((1, 1, tn), lambda i, j: (jnp.minimum(i // per, g - 1), 0, j))],
        out_specs=pl.BlockSpec((tm, tn), lambda i, j: (i, j)),
        out_shape=jax.ShapeDtypeStruct((m, n), F32),
        compiler_params=_cparams("parallel", "arbitrary"),
        name="mm_res2",
    )(a, b, w, w, res, gate)


def _axial_rope_tables(T):
    rows = T // GRID_W
    row, col = jnp.meshgrid(jnp.arange(rows), jnp.arange(GRID_W), indexing='ij')
    inv_freq = ROPE_BASE ** (-jnp.arange(0, AXIS_DIM, 2, dtype=jnp.float32) / AXIS_DIM)

    def angles(pos):
        ang = pos.reshape(-1).astype(jnp.float32)[:, None] * inv_freq
        return jnp.concatenate([ang, ang], axis=-1)

    ang = jnp.concatenate([angles(row), angles(col)], axis=-1)
    return jnp.cos(ang), jnp.sin(ang)


def _moe(hall, groups, gain, shift, scale, gate, rw_t, wg, wu, wd, layer):
    m, d = hall.shape
    ne = rw_t.shape[0]
    set_len0 = groups[0][2]
    hm, aff = _router(hall, gain, shift, scale, rw_t, set_len0, 512)
    idx_parts, gate_parts, meta = [], [], []
    r_off = 0
    for row_off, n_sets, set_len in groups:
        cap = EC_CAPACITY * set_len // ne
        a = aff[:, row_off:row_off + n_sets * set_len].reshape(ne, n_sets, set_len)
        gts, idx = lax.top_k(a, cap)
        rows = idx + (row_off + jnp.arange(n_sets, dtype=jnp.int32) * set_len)[None, :, None]
        idx_parts.append(rows.reshape(ne, n_sets * cap))
        gate_parts.append(gts.reshape(ne, n_sets * cap))
        meta.append((row_off, n_sets, set_len, cap, r_off, idx))
        r_off += n_sets * cap
    rows_per_expert = r_off
    idx_all = jnp.concatenate(idx_parts, axis=1).astype(jnp.int32)
    gates_all = jnp.concatenate(gate_parts, axis=1)
    y = _experts(hm, idx_all.reshape(ne, 1, rows_per_expert), gates_all.reshape(ne, rows_per_expert, 1),
                 wg, wu, wd, layer)
    y2d = y.reshape(ne * rows_per_expert, d)
    e_base = (jnp.arange(ne, dtype=jnp.int32) * rows_per_expert)[None, :, None]
    for gi, (row_off, n_sets, set_len, cap, r0, idx) in enumerate(meta):
        tok = jnp.swapaxes(idx, 0, 1).astype(jnp.int32)
        yrow = e_base + r0 + (jnp.arange(n_sets, dtype=jnp.int32) * cap)[:, None, None] \
            + jnp.arange(cap, dtype=jnp.int32)[None, None, :]
        if gi == 0:
            tabs = _combine_tables(tok.reshape(n_sets, ne * cap), yrow.reshape(n_sets, ne * cap), set_len)
            for s in range(n_sets):
                hall = _combine(hall, y2d, gate[s:s + 1], *(t[s] for t in tabs), row_off + s * set_len, set_len, ne)
        else:
            tok = tok + (jnp.arange(n_sets, dtype=jnp.int32) * set_len)[:, None, None]
            tabs = _combine_tables(tok.reshape(1, -1), yrow.reshape(1, -1), n_sets * set_len)
            hall = _combine(hall, y2d, gate[-1:], *(t[0] for t in tabs), row_off, n_sets * set_len, ne)
    return hall


def kernel(x, c, ctx, c_ctx, mod_w, mod_b, norm_mix, norm_ffn, ab_w_in, gla_gate_w, gla_gate_b, gla_norm,
           sg_norm_w, sg_norm_b, sg_ws, sg_bs, ab_w_out, c_w_in, sinks, c_w_out, router_w, moe_w_gate,
           moe_w_up, moe_w_down, final_norm):
    B, T, D = x.shape
    L = ctx.shape[1]
    ML, MC = B * T, B * L
    cos, sin = _axial_rope_tables(T)

    cond = jnp.concatenate([jax.nn.silu(c), jax.nn.silu(c_ctx)[None], jnp.zeros((3, D), x.dtype)], axis=0)
    hall = jnp.concatenate([x.reshape(ML, D), ctx.reshape(MC, D)], axis=0)

    for layer in range(DEPTH):
        last = layer == DEPTH - 1
        mods = _mm_bias(cond.astype(BF), mod_w, layer, mod_b[layer].reshape(1, N_MOD * D), 512)[:B + 1]
        mods = mods.reshape(B + 1, N_MOD, 1, D)
        mod = lambda i: mods[:, i]

        xall = _norm_mod(hall, norm_mix[layer], mod(0), mod(1), T, 512, BF)

        if layer % 2 == 0:
            e = layer // 2
            w_in = ab_w_in[e]
            w_main = jnp.concatenate([w_in[:, :6144], w_in[:, 6176:]], axis=1).astype(BF)
            w_lr = jnp.pad(w_in[:, 6144:6176], ((0, 0), (0, 96))).astype(BF)
            p = _mm(xall, w_main, 1024, 1024)
            lr = _mm(xall, w_lr, 1024, LANES)
            gw = [jnp.zeros((LANES, GLA_K_W), F32).at[d * GLA_RANK:(d + 1) * GLA_RANK].set(gla_gate_w[e, d]).astype(BF)
                  for d in range(2)]
            o_fwd = _gla(p, lr, gw[0], gla_gate_b[e, 0:1], None, None, B, T, L, reverse=False)
            a_mix = _gla(p, lr, gw[1], gla_gate_b[e, 1:2], o_fwd, gla_norm[e].reshape(1, GLA_V_W), B, T, L,
                         reverse=True)
            b_mix = _spatial_gate(p, sg_norm_w[e].reshape(1, SG_W), sg_norm_b[e].reshape(1, SG_W), sg_ws[e],
                                  sg_bs[e].reshape(SG_GROUPS, SG_CHUNK, 1))
            hall = _mm_res2(a_mix, b_mix, ab_w_out[e].astype(BF), hall, mod(2), T, 1024, 1024)
        else:
            o = layer // 2
            p = _mm(xall, c_w_in[o].astype(BF), 1024, 1024)
            att = _attention(p, cos, sin, sinks[o], B, T, L)
            assert last, "an attention layer that is not the last would also have to update the context stream"
            hall = _mm_res(att, c_w_out[o].astype(BF), hall, mods[:B, 2], T, 1024, 1024)

        rw_t = router_w[layer].T.astype(BF)
        moe_w = (moe_w_gate, moe_w_up, moe_w_down, layer)
        if last:
            hall = _moe(hall, [(0, B, T)], norm_ffn[layer], mods[:B, 3], mods[:B, 4], mods[:B, 5], rw_t, *moe_w)
        else:
            hall = _moe(hall, [(0, B, T), (ML, B, L)], norm_ffn[layer], mod(3), mod(4), mod(5), rw_t, *moe_w)

    zero = jnp.zeros((1, 1, D), jnp.float32)
    out = _norm_mod(hall[:ML], final_norm, zero, zero, ML, 512, jnp.float32)
    return out.reshape(B, T, D)
```

```python
import functools

import jax
import jax.numpy as jnp
import numpy as np
from jax import lax
from jax.experimental import pallas as pl
from jax.experimental.pallas import tpu as pltpu

D_MODEL = 4096
DEPTH = 2
GRID_W = 64
EPS = 1e-6
NEG_INF = -1e30
N_MOD = 6
MIX_W = D_MODEL
GLA_HEADS = 4
GLA_V_W = MIX_W // 2
GLA_DV = GLA_V_W // GLA_HEADS
GLA_K_W = GLA_V_W // 2
GLA_DK = GLA_K_W // GLA_HEADS
GLA_RANK = 16
GLA_TAU = 16.0
GLA_CHUNK = 64
SG_GROUPS = 4
SG_W = MIX_W // 2
SG_GC = SG_W // SG_GROUPS
SG_CHUNK = 128
ATT_HEADS = D_MODEL // 128
KV_HEADS = 8
HEAD_DIM = 128
GQ = ATT_HEADS // KV_HEADS
WINDOW = 128
ATT_BLOCK = 128
AXIS_DIM = HEAD_DIM // 2
ROPE_BASE = 10000.0
Q_W = ATT_HEADS * HEAD_DIM
KV_W = KV_HEADS * HEAD_DIM
N_EXPERTS = 16
EC_CAPACITY = 2

V7X_VMEM_LIMIT_BYTES = 56 * 1024 * 1024
LANES = 128

BF = jnp.bfloat16
F32 = jnp.float32


def _cparams(*sem, vmem_limit_bytes=V7X_VMEM_LIMIT_BYTES):
    return pltpu.CompilerParams(dimension_semantics=sem, vmem_limit_bytes=vmem_limit_bytes)


def _mm_body(x_ref, w_ref, o_ref):
    o_ref[...] = jnp.dot(x_ref[...], w_ref[...], preferred_element_type=F32).astype(o_ref.dtype)


def _mm(x, w, tm, tn, out_dtype=F32):
    m, k = x.shape
    n = w.shape[1]
    assert m % tm == 0 and n % tn == 0, (m, n, tm, tn)
    return pl.pallas_call(
        _mm_body,
        grid=(m // tm, n // tn),
        in_specs=[pl.BlockSpec((tm, k), lambda i, j: (i, 0)),
                  pl.BlockSpec((k, tn), lambda i, j: (0, j))],
        out_specs=pl.BlockSpec((tm, tn), lambda i, j: (i, j)),
        out_shape=jax.ShapeDtypeStruct((m, n), out_dtype),
        compiler_params=_cparams("parallel", "arbitrary"),
        name="mm",
    )(x, w)


def _mm_bias_body(x_ref, w_ref, b_ref, o_ref):
    o_ref[...] = jnp.dot(x_ref[...], w_ref[...].astype(BF), preferred_element_type=F32) + b_ref[...]


def _mm_bias(x, w, layer, bias, tn):
    m, k = x.shape
    n = w.shape[2]
    assert n % tn == 0
    return pl.pallas_call(
        _mm_bias_body,
        grid=(n // tn,),
        in_specs=[pl.BlockSpec((m, k), lambda j: (0, 0)),
                  pl.BlockSpec((None, k, tn), lambda j: (layer, 0, j)),
                  pl.BlockSpec((1, tn), lambda j: (0, j))],
        out_specs=pl.BlockSpec((m, tn), lambda j: (0, j)),
        out_shape=jax.ShapeDtypeStruct((m, n), F32),
        compiler_params=_cparams("parallel"),
        name="mm_bias",
    )(x, w, bias)


def _mm_res_body(x_ref, w_ref, r_ref, g_ref, o_ref):
    acc = jnp.dot(x_ref[...], w_ref[...], preferred_element_type=F32)
    o_ref[...] = r_ref[...] + g_ref[0] * acc


def _mm_res(x, w, res, gate, rows_per_group, tm, tn):
    m, k = x.shape
    n = w.shape[1]
    g = gate.shape[0]
    per = rows_per_group // tm
    assert m % tm == 0 and n % tn == 0 and rows_per_group % tm == 0
    return pl.pallas_call(
        _mm_res_body,
        grid=(m // tm, n // tn),
        in_specs=[pl.BlockSpec((tm, k), lambda i, j: (i, 0)),
                  pl.BlockSpec((k, tn), lambda i, j: (0, j)),
                  pl.BlockSpec((tm, tn), lambda i, j: (i, j)),
                  pl.BlockSpec((1, 1, tn), lambda i, j: (jnp.minimum(i // per, g - 1), 0, j))],
        out_specs=pl.BlockSpec((tm, tn), lambda i, j: (i, j)),
        out_shape=jax.ShapeDtypeStruct((m, n), F32),
        compiler_params=_cparams("parallel", "arbitrary"),
        name="mm_res",
    )(x, w, res, gate)


def _modulated(x, gain, shift, scale):
    y = x * lax.rsqrt(jnp.mean(x * x, axis=-1, keepdims=True) + EPS)
    return (y * gain) * (1.0 + scale) + shift


def _norm_mod_body(x_ref, g_ref, sh_ref, sc_ref, o_ref):
    o_ref[...] = _modulated(x_ref[...], g_ref[...], sh_ref[0], sc_ref[0]).astype(o_ref.dtype)


def _group_index(rows_per_group, tm, groups):
    per = rows_per_group // tm
    assert rows_per_group % tm == 0
    return lambda i: (jnp.minimum(i // per, groups - 1), 0, 0)


def _norm_mod(h, gain, shift, scale, rows_per_group, tm, out_dtype):
    m, d = h.shape
    grp = _group_index(rows_per_group, tm, shift.shape[0])
    return pl.pallas_call(
        _norm_mod_body,
        grid=(m // tm,),
        in_specs=[pl.BlockSpec((tm, d), lambda i: (i, 0)),
                  pl.BlockSpec((1, d), lambda i: (0, 0)),
                  pl.BlockSpec((1, 1, d), grp),
                  pl.BlockSpec((1, 1, d), grp)],
        out_specs=pl.BlockSpec((tm, d), lambda i: (i, 0)),
        out_shape=jax.ShapeDtypeStruct((m, d), out_dtype),
        compiler_params=_cparams("parallel"),
        name="norm_mod",
    )(h, gain.reshape(1, d), shift, scale)


def _router_body(x_ref, g_ref, sh_ref, sc_ref, rwt_ref, hm_ref, aff_ref):
    hm = _modulated(x_ref[...], g_ref[...], sh_ref[0], sc_ref[0])
    hm_ref[...] = hm
    logits = lax.dot_general(rwt_ref[...], hm.astype(BF), (((1,), (1,)), ((), ())),
                             preferred_element_type=F32)
    e = jnp.exp(logits - jnp.max(logits, axis=0, keepdims=True))
    aff_ref[...] = e / jnp.sum(e, axis=0, keepdims=True)


def _router(h, gain, shift, scale, rw_t, rows_per_group, tm):
    m, d = h.shape
    ne = rw_t.shape[0]
    grp = _group_index(rows_per_group, tm, shift.shape[0])
    return pl.pallas_call(
        _router_body,
        grid=(m // tm,),
        in_specs=[pl.BlockSpec((tm, d), lambda i: (i, 0)),
                  pl.BlockSpec((1, d), lambda i: (0, 0)),
                  pl.BlockSpec((1, 1, d), grp),
                  pl.BlockSpec((1, 1, d), grp),
                  pl.BlockSpec((ne, d), lambda i: (0, 0))],
        out_specs=[pl.BlockSpec((tm, d), lambda i: (i, 0)),
                   pl.BlockSpec((ne, tm), lambda i: (0, i))],
        out_shape=[jax.ShapeDtypeStruct((m, d), F32), jax.ShapeDtypeStruct((ne, m), F32)],
        compiler_params=_cparams("parallel"),
        name="router",
    )(h, gain.reshape(1, d), shift, scale, rw_t)


GATHER_ROWS = 128
GATHER_BUFS = 3
EXPERT_K_SPLIT = 2
EXPERTS_VMEM_LIMIT_BYTES = 60 * 1024 * 1024


def _experts_body(idx_vm, idx_next_vm, gates_ref, hm_hbm, wg_ref, wu_ref, wd_ref, y_ref,
                  idx_sm, xg, hid, g_acc, u_acc, stage, sem, *, rows, mm_rows, n_ff_tiles, n_d_tiles):
    e = pl.program_id(0)
    j = pl.program_id(1)
    n_experts = pl.num_programs(0)
    ch, nb, ks = GATHER_ROWS, GATHER_BUFS, EXPERT_K_SPLIT
    n_chunks = rows // ch
    tf = hid.shape[2]
    kd = xg.shape[2]
    n_up, n_down = n_ff_tiles * ks, n_d_tiles * ks

    def issue(q):
        def body(r, carry):
            src = idx_sm[0, q * ch + r]
            pltpu.make_async_copy(hm_hbm.at[pl.ds(src, 1)], stage.at[q % nb, pl.ds(r, 1)], sem.at[q % nb]).start()
            return carry
        lax.fori_loop(0, ch, body, 0, unroll=8)

    def finish(q):
        pltpu.make_async_copy(hm_hbm.at[pl.ds(0, ch)], stage.at[q % nb], sem.at[q % nb]).wait()
        for kh in range(ks):
            xg[kh, q * ch:(q + 1) * ch, :] = stage[q % nb, :, kh * kd:(kh + 1) * kd].astype(BF)

    def gather(lo, hi, first):
        if first:
            for q in range(min(nb, n_chunks)):
                issue(q)
        for q in range(lo, hi):
            finish(q)
            if q + nb < n_chunks:
                issue(q + nb)

    @pl.when((e == 0) & (j == 0))
    def _first_gather():
        pltpu.sync_copy(idx_vm.at[0], idx_sm)
        gather(0, n_chunks, True)

    for k in range(n_down):
        lo, hi = k * n_chunks // n_down, (k + 1) * n_chunks // n_down
        if k == 0 or hi > lo:
            @pl.when((j == n_up + k) & (e + 1 < n_experts))
            def _next_gather(k=k, lo=lo, hi=hi):
                if k == 0:
                    pltpu.sync_copy(idx_next_vm.at[0], idx_sm)
                gather(lo, hi, k == 0)

    @pl.when(j < n_up)
    def _up():
        f, kh = j // ks, j % ks
        w = jnp.concatenate([wg_ref[0, 0, 0].astype(BF), wu_ref[0, 0, 0].astype(BF)], axis=1)
        for rc in range(rows // mm_rows):
            rs = slice(rc * mm_rows, (rc + 1) * mm_rows)
            gu = jnp.dot(xg[kh, rs, :], w, preferred_element_type=F32)
            g, u = gu[:, :tf], gu[:, tf:]

            @pl.when(kh == 0)
            def _start():
                g_acc[rs, :] = g
                u_acc[rs, :] = u

            @pl.when((kh > 0) & (kh < ks - 1))
            def _accumulate():
                g_acc[rs, :] += g
                u_acc[rs, :] += u

            @pl.when(kh == ks - 1)
            def _finish():
                gt = g_acc[rs, :] + g
                ut = u_acc[rs, :] + u
                hid[f, rs, :] = ((gt * jax.nn.sigmoid(gt)) * ut).astype(BF)

    @pl.when(j >= n_up)
    def _down():
        kh = (j - n_up) % ks
        per = n_ff_tiles // ks
        wd = wd_ref[0, 0, 0].astype(BF)
        for rc in range(rows // mm_rows):
            rs = slice(rc * mm_rows, (rc + 1) * mm_rows)
            acc = jnp.dot(hid[kh * per, rs, :], wd[0:tf, :], preferred_element_type=F32)
            for f in range(1, per):
                acc = acc + jnp.dot(hid[kh * per + f, rs, :], wd[f * tf:(f + 1) * tf, :], preferred_element_type=F32)
            acc = acc * gates_ref[0, rs, :]

            @pl.when(kh == 0)
            def _start():
                y_ref[0, rs, :] = acc

            @pl.when(kh > 0)
            def _accumulate():
                y_ref[0, rs, :] += acc


def _experts(hm, idx, gates, wg, wu, wd, layer, tf=256, td=512):
    ks = EXPERT_K_SPLIT
    ne, _, rows = idx.shape
    d = hm.shape[1]
    ff = wg.shape[3]
    n_ff_tiles, n_d_tiles = ff // tf, d // td
    n_up, n_down = n_ff_tiles * ks, n_d_tiles * ks
    mm_rows = rows
    assert rows % GATHER_ROWS == 0 and mm_rows % 8 == 0 and n_ff_tiles % ks == 0 and d % ks == 0
    body = functools.partial(_experts_body, rows=rows, mm_rows=mm_rows, n_ff_tiles=n_ff_tiles, n_d_tiles=n_d_tiles)
    up = lambda j: jnp.minimum(j, n_up - 1)
    down = lambda j: jnp.maximum(j - n_up, 0)
    wg4 = wg.reshape(wg.shape[0], ne, ks, d // ks, ff)
    wu4 = wu.reshape(wu.shape[0], ne, ks, d // ks, ff)
    wd4 = wd.reshape(wd.shape[0], ne, ks, ff // ks, d)
    up_spec = pl.BlockSpec((1, 1, 1, d // ks, tf), lambda e, j: (layer, e, up(j) % ks, 0, up(j) // ks))
    return pl.pallas_call(
        body,
        grid=(ne, n_up + n_down),
        in_specs=[pl.BlockSpec((1, 1, rows), lambda e, j: (e, 0, 0)),
                  pl.BlockSpec((1, 1, rows), lambda e, j: (jnp.minimum(e + 1, ne - 1), 0, 0)),
                  pl.BlockSpec((1, rows, 1), lambda e, j: (e, 0, 0)),
                  pl.BlockSpec(memory_space=pl.ANY),
                  up_spec,
                  up_spec,
                  pl.BlockSpec((1, 1, 1, ff // ks, td), lambda e, j: (layer, e, down(j) % ks, 0, down(j) // ks))],
        out_specs=pl.BlockSpec((1, rows, td), lambda e, j: (e, 0, down(j) // ks)),
        out_shape=jax.ShapeDtypeStruct((ne, rows, d), F32),
        scratch_shapes=[pltpu.SMEM((1, rows), jnp.int32),
                        pltpu.VMEM((ks, rows, d // ks), BF),
                        pltpu.VMEM((n_ff_tiles, rows, tf), BF),
                        pltpu.VMEM((rows, tf), F32),
                        pltpu.VMEM((rows, tf), F32),
                        pltpu.VMEM((GATHER_BUFS, GATHER_ROWS, d), F32),
                        pltpu.SemaphoreType.DMA((GATHER_BUFS,))],
        compiler_params=_cparams("arbitrary", "arbitrary", vmem_limit_bytes=EXPERTS_VMEM_LIMIT_BYTES),
        name="experts",
    )(idx, idx, gates, hm, wg4, wu4, wd4)


COMBINE_TOKENS = 32
COMBINE_BUFS = 4
COMBINE_STRIP = 512
COMBINE_UNROLL = 4


def _combine_body(pk_ref, ts_ref, mc_ref, h_ref, cnt_ref, gate_ref, y_hbm, *rest, n_tiles, n_slots, epilogue):
    if epilogue == "modulate":
        ng_ref, nsh_ref, nsc_ref, _, o_ref, x_ref, buf, sem = rest
    elif epilogue == "norm":
        ng_ref, o_ref, buf, sem = rest
    else:
        o_ref, buf, sem = rest
    g = pl.program_id(0)
    tt, d = h_ref.shape
    nb = buf.shape[0]

    def issue(t):
        slot = t % nb

        def start(p):
            v = pk_ref[p]
            src = jnp.bitwise_and(v, 0xFFFF)
            dst = jnp.right_shift(v, 16)
            pltpu.make_async_copy(y_hbm.at[pl.ds(src, 1)], buf.at[slot, pl.ds(dst, 1)], sem.at[slot]).start()

        lo, hi = ts_ref[t], ts_ref[t + 1]
        n_groups = (hi - lo) // COMBINE_UNROLL

        def group(i, carry):
            for u in range(COMBINE_UNROLL):
                start(lo + i * COMBINE_UNROLL + u)
            return carry
        lax.fori_loop(0, n_groups, group, 0)

        def single(p, carry):
            start(p)
            return carry
        lax.fori_loop(lo + n_groups * COMBINE_UNROLL, hi, single, 0)

    def drain(t):
        slot = t % nb
        n = ts_ref[t + 1] - ts_ref[t]
        for bit in range((n_slots * tt).bit_length()):
            @pl.when(jnp.bitwise_and(n, 1 << bit) != 0)
            def _wait(bit=bit):
                rows = pl.ds(0, 1 << bit)
                pltpu.make_async_copy(y_hbm.at[rows], buf.at[slot, rows], sem.at[slot]).wait()

    @pl.when(g == 0)
    def _first():
        def zero(r, carry):
            buf[:, pl.ds(pl.multiple_of(r * tt, tt), tt), :] = jnp.zeros((nb, tt, d), F32)
            return carry
        lax.fori_loop(0, n_slots, zero, 0)
        for t in range(min(nb - 1, n_tiles)):
            issue(t)

    @pl.when(g + nb - 1 < n_tiles)
    def _prefetch():
        issue(g + nb - 1)

    slot = g % nb
    drain(g)
    n_used = mc_ref[g]
    cnt = jnp.broadcast_to(cnt_ref[...], (tt, COMBINE_STRIP))
    for cs in range(d // COMBINE_STRIP):
        cols = slice(cs * COMBINE_STRIP, (cs + 1) * COMBINE_STRIP)

        def body(k, acc):
            v = buf[slot, pl.ds(pl.multiple_of(k * tt, tt), tt), cols]
            return acc + jnp.where(cnt > k, v, 0.0)

        acc = lax.fori_loop(0, n_used, body, jnp.zeros((tt, COMBINE_STRIP), F32))
        o_ref[:, cols] = h_ref[:, cols] + gate_ref[0][:, cols] * acc

    if epilogue == "modulate":
        x_ref[...] = _modulated(o_ref[...], ng_ref[...], nsh_ref[0], nsc_ref[0]).astype(x_ref.dtype)
    elif epilogue == "norm":
        hn = o_ref[...]
        o_ref[...] = (hn * lax.rsqrt(jnp.mean(hn * hn, axis=-1, keepdims=True) + EPS)) * ng_ref[...]


def _combine_tables(tok, yrow, n_tok):
    tt = COMBINE_TOKENS
    s, n_pairs = tok.shape
    key = jnp.sort(jnp.bitwise_or(jnp.left_shift(tok, 16), yrow), axis=-1)
    tok_s = jnp.right_shift(key, 16)
    src = jnp.bitwise_and(key, 0xFFFF)
    cnt = jnp.sum(tok[:, :, None] == jnp.arange(n_tok, dtype=jnp.int32)[None, None, :], axis=1, dtype=jnp.int32)
    start = jnp.concatenate([jnp.zeros((s, 1), jnp.int32), jnp.cumsum(cnt, axis=1, dtype=jnp.int32)], axis=1)
    pos = jnp.broadcast_to(jnp.arange(n_pairs, dtype=jnp.int32), (s, n_pairs))
    is_first = jnp.concatenate([jnp.ones((s, 1), bool), tok_s[:, 1:] != tok_s[:, :-1]], axis=1)
    k = pos - lax.cummax(jnp.where(is_first, pos, 0), axis=1)
    packed = jnp.bitwise_or(src, jnp.left_shift(k * tt + tok_s % tt, 16))
    return packed, start[:, ::tt], jnp.max(cnt.reshape(s, n_tok // tt, tt), axis=2), cnt


def _combine(h, y2d, gate, packed, tile_start, tile_slots, cnt, row_off, n_slots, norm_gain=None, modulate=None):
    tt = COMBINE_TOKENS
    d = h.shape[1]
    n_tok = cnt.shape[0]
    n_tiles = n_tok // tt
    assert n_tok % tt == 0 and row_off % tt == 0 and y2d.shape[0] <= 0x10000 and n_slots * tt <= 0x7FFF
    assert gate.shape[0] == 1 and (norm_gain is None or modulate is None)
    tile0 = row_off // tt
    rows_spec = pl.BlockSpec((tt, d), lambda g, *_: (tile0 + g, 0))
    vec_spec = pl.BlockSpec((1, d), lambda g, *_: (0, 0))
    vec3_spec = pl.BlockSpec((1, 1, d), lambda g, *_: (0, 0, 0))
    in_specs = [rows_spec, pl.BlockSpec((tt, 1), lambda g, *_: (g, 0)), vec3_spec, pl.BlockSpec(memory_space=pl.ANY)]
    args = [packed, tile_start, tile_slots, h, cnt.reshape(n_tok, 1), gate, y2d]
    out_specs, out_shape, aliases, epilogue = rows_spec, jax.ShapeDtypeStruct(h.shape, F32), {3: 0}, None
    if norm_gain is not None:
        epilogue = "norm"
        in_specs += [vec_spec]
        args += [norm_gain]
    elif modulate is not None:
        epilogue = "modulate"
        gain, shift, scale, x = modulate
        in_specs += [vec_spec, vec3_spec, vec3_spec, pl.BlockSpec(memory_space=pl.ANY)]
        args += [gain, shift, scale, x]
        out_specs = [rows_spec, rows_spec]
        out_shape = [out_shape, jax.ShapeDtypeStruct(x.shape, x.dtype)]
        aliases = {3: 0, len(args) - 1: 1}
    grid_spec = pltpu.PrefetchScalarGridSpec(
        num_scalar_prefetch=3,
        grid=(n_tiles,),
        in_specs=in_specs,
        out_specs=out_specs,
        scratch_shapes=[pltpu.VMEM((COMBINE_BUFS, n_slots * tt, d), F32),
                        pltpu.SemaphoreType.DMA((COMBINE_BUFS,))],
    )
    return pl.pallas_call(
        functools.partial(_combine_body, n_tiles=n_tiles, n_slots=n_slots, epilogue=epilogue),
        grid_spec=grid_spec,
        out_shape=out_shape,
        input_output_aliases=aliases,
        compiler_params=_cparams("arbitrary"),
        name="combine",
    )(*args)


ATT_Q_TILE = 512


def _rope(x, cos, sin):
    lane = lax.broadcasted_iota(jnp.int32, x.shape, 1)
    fwd = pltpu.roll(x, AXIS_DIM // 2, axis=1)
    bwd = pltpu.roll(x, HEAD_DIM - AXIS_DIM // 2, axis=1)
    rot = jnp.where(lane % AXIS_DIM < AXIS_DIM // 2, -bwd, fwd)
    return x * cos + rot * sin


def _attn_body(q_ref, kp_ref, km_ref, kn_ref, vp_ref, vm_ref, vn_ref, kc_ref, vc_ref,
               cm_ref, sm_ref, cp_ref, sp_ref, cn_ref, sn_ref, sink_ref, o_ref, *, n_q_tiles):
    i = pl.program_id(2)
    qt = q_ref.shape[0]
    blk = ATT_BLOCK
    nb = qt // blk
    scale = HEAD_DIM ** -0.5
    cos_w = jnp.concatenate([cp_ref[...], cm_ref[...], cn_ref[...]], axis=0)
    sin_w = jnp.concatenate([sp_ref[...], sm_ref[...], sn_ref[...]], axis=0)
    k_w = jnp.concatenate([kp_ref[...], km_ref[...], kn_ref[...]], axis=0)
    k_w = _rope(k_w, cos_w, sin_w).astype(BF)
    v_w = jnp.concatenate([vp_ref[...], vm_ref[...], vn_ref[...]], axis=0).astype(BF)
    kc = kc_ref[...].astype(BF)
    vc = vc_ref[...].astype(BF)
    sink = sink_ref[0]
    nt = (((1,), (1,)), ((), ()))
    band = 3 * blk
    row = lax.broadcasted_iota(jnp.int32, (GQ * blk, band), 0) % blk
    col = lax.broadcasted_iota(jnp.int32, (GQ * blk, band), 1)
    in_window = (col >= row) & (col <= row + 2 * WINDOW)
    for jb in range(nb):
        rs = slice(jb * blk, (jb + 1) * blk)
        cos_q = cm_ref[rs, :]
        sin_q = sm_ref[rs, :]
        qs = jnp.concatenate([_rope(q_ref[rs, g * HEAD_DIM:(g + 1) * HEAD_DIM], cos_q, sin_q)
                              for g in range(GQ)], axis=0).astype(BF)
        kb = k_w[jb * blk:jb * blk + band]
        vb = v_w[jb * blk:jb * blk + band]
        s_loc = lax.dot_general(qs, kb, nt, preferred_element_type=F32) * scale
        s_ctx = lax.dot_general(qs, kc, nt, preferred_element_type=F32) * scale
        valid = in_window
        if jb == 0:
            valid = valid & (col >= jnp.where(i == 0, blk, 0))
        if jb == nb - 1:
            valid = valid & (col < jnp.where(i == n_q_tiles - 1, 2 * blk, band))
        s_loc = jnp.where(valid, s_loc, NEG_INF)
        m = jnp.maximum(jnp.maximum(jnp.max(s_loc, axis=1, keepdims=True),
                                    jnp.max(s_ctx, axis=1, keepdims=True)), sink)
        e_loc = jnp.exp(s_loc - m)
        e_ctx = jnp.exp(s_ctx - m)
        den = (jnp.sum(e_loc, axis=1, keepdims=True) + jnp.sum(e_ctx, axis=1, keepdims=True)
               + jnp.exp(sink - m))
        o = (jnp.dot(e_loc.astype(BF), vb, preferred_element_type=F32)
             + jnp.dot(e_ctx.astype(BF), vc, preferred_element_type=F32)) / den
        for g in range(GQ):
            o_ref[rs, g * HEAD_DIM:(g + 1) * HEAD_DIM] = o[g * blk:(g + 1) * blk].astype(o_ref.dtype)


def _attention(p, cos, sin, sinks, n_batch, seq, ctx_len):
    qt, blk = ATT_Q_TILE, ATT_BLOCK
    ml = n_batch * seq
    nq = seq // qt
    nbk = seq // blk
    bpt = qt // blk
    kcol, vcol = Q_W // HEAD_DIM, (Q_W + KV_W) // HEAD_DIM
    assert seq % qt == 0 and ml % ctx_len == 0
    sink_col = jnp.repeat(sinks.astype(F32).reshape(KV_HEADS, GQ, 1), blk, axis=2).reshape(KV_HEADS, GQ * blk, 1)
    prev_blk = lambda i: jnp.maximum(i * bpt - 1, 0)
    next_blk = lambda i: jnp.minimum((i + 1) * bpt, nbk - 1)
    qspec = pl.BlockSpec((qt, GQ * HEAD_DIM), lambda b, h, i: (b * nq + i, h))

    def kv_specs(c0):
        return [pl.BlockSpec((blk, HEAD_DIM), lambda b, h, i: (b * nbk + prev_blk(i), c0 + h)),
                pl.BlockSpec((qt, HEAD_DIM), lambda b, h, i: (b * nq + i, c0 + h)),
                pl.BlockSpec((blk, HEAD_DIM), lambda b, h, i: (b * nbk + next_blk(i), c0 + h))]

    ctx_spec = lambda c0: pl.BlockSpec((ctx_len, HEAD_DIM), lambda b, h, i: (ml // ctx_len + b, c0 + h))
    tab_specs = [pl.BlockSpec((qt, HEAD_DIM), lambda b, h, i: (i, 0)),
                 pl.BlockSpec((qt, HEAD_DIM), lambda b, h, i: (i, 0)),
                 pl.BlockSpec((blk, HEAD_DIM), lambda b, h, i: (prev_blk(i), 0)),
                 pl.BlockSpec((blk, HEAD_DIM), lambda b, h, i: (prev_blk(i), 0)),
                 pl.BlockSpec((blk, HEAD_DIM), lambda b, h, i: (next_blk(i), 0)),
                 pl.BlockSpec((blk, HEAD_DIM), lambda b, h, i: (next_blk(i), 0))]
    return pl.pallas_call(
        functools.partial(_attn_body, n_q_tiles=nq),
        grid=(n_batch, KV_HEADS, nq),
        in_specs=[qspec] + kv_specs(kcol) + kv_specs(vcol) + [ctx_spec(kcol), ctx_spec(vcol)] + tab_specs
                 + [pl.BlockSpec((1, GQ * blk, 1), lambda b, h, i: (h, 0, 0))],
        out_specs=pl.BlockSpec((qt, GQ * HEAD_DIM), lambda b, h, i: (b * nq + i, h)),
        out_shape=jax.ShapeDtypeStruct((ml, Q_W), BF),
        compiler_params=_cparams("parallel", "parallel", "arbitrary"),
        name="attention",
    )(p, p, p, p, p, p, p, p, p, cos, sin, cos, sin, cos, sin, sink_col)


GLA_TOKENS = 256
_NT = (((1,), (1,)), ((), ()))
_TN = (((0,), (0,)), ((), ()))


def _gla_body(*refs, reverse, fuse_output):
    if fuse_output:
        q_ref, k_ref, v_ref, lr_ref, gw_ref, gb_ref, of_ref, g_ref, gain_ref, o_ref, st_ref = refs
    else:
        q_ref, k_ref, v_ref, lr_ref, gw_ref, gb_ref, o_ref, st_ref = refs
    tb = q_ref.shape[0]
    ch = GLA_CHUNK
    n_ch = tb // ch

    @pl.when(pl.program_id(1) == 0)
    def _reset():
        st_ref[...] = jnp.zeros(st_ref.shape, F32)

    r = lax.broadcasted_iota(jnp.int32, (tb, tb), 0)
    c = lax.broadcasted_iota(jnp.int32, (tb, tb), 1)
    same = (r // ch) == (c // ch)
    tri = same & ((c >= r) if reverse else (c <= r))
    tri_bf = jnp.where(tri, 1.0, 0.0).astype(BF)
    same_bf = jnp.where(same, 1.0, 0.0).astype(BF)

    z = jnp.dot(lr_ref[...].astype(BF), gw_ref[...], preferred_element_type=F32) + gb_ref[...]
    la = jax.nn.log_sigmoid(z) * (1.0 / GLA_TAU)
    la_hi = la.astype(BF)
    la_lo = (la - la_hi.astype(F32)).astype(BF)
    b = (jnp.dot(tri_bf, la_hi, preferred_element_type=F32)
         + jnp.dot(tri_bf, la_lo, preferred_element_type=F32))
    tot = (jnp.dot(same_bf, la_hi, preferred_element_type=F32)
           + jnp.dot(same_bf, la_lo, preferred_element_type=F32))
    q = q_ref[...] * (GLA_DK ** -0.5)
    k = k_ref[...]
    v = v_ref[...].astype(BF)
    q_in = (q * jnp.exp(b)).astype(BF)
    k_in = (k * jnp.exp(-b)).astype(BF)
    k_dec = (k * jnp.exp(tot - b)).astype(BF)
    dec = jnp.exp(tot)
    for h in range(GLA_HEADS):
        ks = slice(h * GLA_DK, (h + 1) * GLA_DK)
        vs = slice(h * GLA_DV, (h + 1) * GLA_DV)
        sc = lax.dot_general(q_in[:, ks], k_in[:, ks], _NT, preferred_element_type=F32)
        sc = jnp.where(tri, sc, 0.0).astype(BF)
        o_intra = jnp.dot(sc, v[:, vs], preferred_element_type=F32)
        outs = [None] * n_ch
        for ci in (range(n_ch - 1, -1, -1) if reverse else range(n_ch)):
            rs = slice(ci * ch, (ci + 1) * ch)
            st = st_ref[h]
            outs[ci] = o_intra[rs] + lax.dot_general(q_in[rs, ks], st.astype(BF), _NT, preferred_element_type=F32)
            upd = lax.dot_general(v[rs, vs], k_dec[rs, ks], _TN, preferred_element_type=F32)
            st_ref[h] = st * dec[ci * ch:ci * ch + 1, ks] + upd
        o = jnp.concatenate(outs, axis=0)
        if fuse_output:
            o = o + of_ref[:, vs]
            y = o * lax.rsqrt(jnp.mean(o * o, axis=-1, keepdims=True) + EPS) * gain_ref[:, vs]
            g = g_ref[:, vs]
            o_ref[:, vs] = (y * (g * jax.nn.sigmoid(g))).astype(o_ref.dtype)
        else:
            o_ref[:, vs] = o


def _gla(p, lr, gate_w, gate_b, o_fwd, gain, n_batch, seq, ctx_len, *, reverse):
    tb = GLA_TOKENS
    ml = n_batch * seq
    m = p.shape[0]
    nc, nl = ctx_len // tb, seq // tb
    assert ctx_len % tb == 0 and seq % tb == 0 and ml % tb == 0
    ctx_base = ml // tb

    def row(b, s):
        sc_ = (nc - 1 - s) if reverse else s
        sl_ = (nl - 1 - (s - nc)) if reverse else (s - nc)
        return jnp.where(s < nc, ctx_base + b * nc + sc_, b * nl + sl_)

    kk, kv, kg = 1, 2 * GLA_K_W // GLA_V_W, (2 * GLA_K_W + GLA_V_W) // GLA_V_W
    in_specs = [pl.BlockSpec((tb, GLA_K_W), lambda b, s: (row(b, s), 0)),
                pl.BlockSpec((tb, GLA_K_W), lambda b, s: (row(b, s), kk)),
                pl.BlockSpec((tb, GLA_V_W), lambda b, s: (row(b, s), kv)),
                pl.BlockSpec((tb, LANES), lambda b, s: (row(b, s), 0)),
                pl.BlockSpec((LANES, GLA_K_W), lambda b, s: (0, 0)),
                pl.BlockSpec((1, GLA_K_W), lambda b, s: (0, 0))]
    args = [p, p, p, lr, gate_w, gate_b]
    if reverse:
        in_specs += [pl.BlockSpec((tb, GLA_V_W), lambda b, s: (row(b, s), 0)),
                     pl.BlockSpec((tb, GLA_V_W), lambda b, s: (row(b, s), kg)),
                     pl.BlockSpec((1, GLA_V_W), lambda b, s: (0, 0))]
        args += [o_fwd, p, gain]
    return pl.pallas_call(
        functools.partial(_gla_body, reverse=reverse, fuse_output=reverse),
        grid=(n_batch, nc + nl),
        in_specs=in_specs,
        out_specs=pl.BlockSpec((tb, GLA_V_W), lambda b, s: (row(b, s), 0)),
        out_shape=jax.ShapeDtypeStruct((m, GLA_V_W), BF if reverse else F32),
        scratch_shapes=[pltpu.VMEM((GLA_HEADS, GLA_DV, GLA_DK), F32)],
        compiler_params=_cparams("parallel", "arbitrary"),
        name="gla_bwd" if reverse else "gla_fwd",
    )(*args)


SG_TOKENS = 512


def _sg_body(u_ref, vs_ref, nw_ref, nb_ref, ws_ref, bs_ref, o_ref):
    u = jax.nn.gelu(u_ref[...])
    vf = jax.nn.gelu(vs_ref[...])
    mu = jnp.mean(vf, axis=-1, keepdims=True)
    var = jnp.mean(jnp.square(vf - mu), axis=-1, keepdims=True)
    vn = ((vf - mu) * lax.rsqrt(var + EPS) * nw_ref[...] + nb_ref[...]).astype(BF)
    ws = ws_ref[0].astype(BF)
    bias = bs_ref[0]
    for ci in range(u.shape[0] // SG_CHUNK):
        rs = slice(ci * SG_CHUNK, (ci + 1) * SG_CHUNK)
        s = jnp.dot(ws, vn[rs], preferred_element_type=F32) + bias
        o_ref[rs, :] = (u[rs] * s).astype(o_ref.dtype)


def _spatial_gate(p, nw, nb, ws, bias_col):
    m = p.shape[0]
    tt = SG_TOKENS
    ku = (2 * GLA_K_W + 2 * GLA_V_W) // SG_GC
    kv = ku + SG_W // SG_GC
    return pl.pallas_call(
        _sg_body,
        grid=(m // tt, SG_GROUPS),
        in_specs=[pl.BlockSpec((tt, SG_GC), lambda i, g: (i, ku + g)),
                  pl.BlockSpec((tt, SG_GC), lambda i, g: (i, kv + g)),
                  pl.BlockSpec((1, SG_GC), lambda i, g: (0, g)),
                  pl.BlockSpec((1, SG_GC), lambda i, g: (0, g)),
                  pl.BlockSpec((1, SG_CHUNK, SG_CHUNK), lambda i, g: (g, 0, 0)),
                  pl.BlockSpec((1, SG_CHUNK, 1), lambda i, g: (g, 0, 0))],
        out_specs=pl.BlockSpec((tt, SG_GC), lambda i, g: (i, g)),
        out_shape=jax.ShapeDtypeStruct((m, SG_W), BF),
        compiler_params=_cparams("parallel", "arbitrary"),
        name="spatial_gate",
    )(p, p, nw, nb, ws, bias_col)


def _mm_res2_body(a_ref, b_ref, wa_ref, wb_ref, r_ref, g_ref, o_ref):
    acc = (jnp.dot(a_ref[...], wa_ref[...], preferred_element_type=F32)
           + jnp.dot(b_ref[...], wb_ref[...], preferred_element_type=F32))
    o_ref[...] = r_ref[...] + g_ref[0] * acc


def _mm_res2(a, b, w, res, gate, rows_per_group, tm, tn):
    m, ka = a.shape
    kb = b.shape[1]
    n = w.shape[1]
    g = gate.shape[0]
    per = rows_per_group // tm
    assert ka == kb and w.shape[0] == ka + kb and m % tm == 0 and n % tn == 0 and rows_per_group % tm == 0
    return pl.pallas_call(
        _mm_res2_body,
        grid=(m // tm, n // tn),
        in_specs=[pl.BlockSpec((tm, ka), lambda i, j: (i, 0)),
                  pl.BlockSpec((tm, kb), lambda i, j: (i, 0)),
                  pl.BlockSpec((ka, tn), lambda i, j: (0, j)),
                  pl.BlockSpec((kb, tn), lambda i, j: (1, j)),
                  pl.BlockSpec((tm, tn), lambda i, j: (i, j)),
                  pl.BlockSpec((1, 1, tn), lambda i, j: (jnp.minimum(i // per, g - 1), 0, j))],
        out_specs=pl.BlockSpec((tm, tn), lambda i, j: (i, j)),
        out_shape=jax.ShapeDtypeStruct((m, n), F32),
        compiler_params=_cparams("parallel", "arbitrary"),
        name="mm_res2",
    )(a, b, w, w, res, gate)


def _axial_rope_tables(T):
    rows = T // GRID_W
    row, col = jnp.meshgrid(jnp.arange(rows), jnp.arange(GRID_W), indexing='ij')
    inv_freq = ROPE_BASE ** (-jnp.arange(0, AXIS_DIM, 2, dtype=jnp.float32) / AXIS_DIM)

    def angles(pos):
        ang = pos.reshape(-1).astype(jnp.float32)[:, None] * inv_freq
        return jnp.concatenate([ang, ang], axis=-1)

    ang = jnp.concatenate([angles(row), angles(col)], axis=-1)
    return jnp.cos(ang), jnp.sin(ang)


def _moe(hall, groups, gain, shift, scale, gate, rw_t, wg, wu, wd, layer, norm_gain=None, next_mod=None, x_next=None):
    m, d = hall.shape
    ne = rw_t.shape[0]
    set_len0 = groups[0][2]
    hm, aff = _router(hall, gain, shift, scale, rw_t, set_len0, 512)
    idx_parts, gate_parts, meta = [], [], []
    r_off = 0
    for row_off, n_sets, set_len in groups:
        cap = EC_CAPACITY * set_len // ne
        a = aff[:, row_off:row_off + n_sets * set_len].reshape(ne, n_sets, set_len)
        gts, idx = lax.top_k(a, cap)
        rows = idx + (row_off + jnp.arange(n_sets, dtype=jnp.int32) * set_len)[None, :, None]
        idx_parts.append(rows.reshape(ne, n_sets * cap))
        gate_parts.append(gts.reshape(ne, n_sets * cap))
        meta.append((row_off, n_sets, set_len, cap, r_off, idx))
        r_off += n_sets * cap
    rows_per_expert = r_off
    idx_all = jnp.concatenate(idx_parts, axis=1).astype(jnp.int32)
    gates_all = jnp.concatenate(gate_parts, axis=1)
    y = _experts(hm, idx_all.reshape(ne, 1, rows_per_expert), gates_all.reshape(ne, rows_per_expert, 1),
                 wg, wu, wd, layer)
    y2d = y.reshape(ne * rows_per_expert, d)
    e_base = (jnp.arange(ne, dtype=jnp.int32) * rows_per_expert)[None, :, None]
    for gi, (row_off, n_sets, set_len, cap, r0, idx) in enumerate(meta):
        tok = jnp.swapaxes(idx, 0, 1).astype(jnp.int32)
        yrow = e_base + r0 + (jnp.arange(n_sets, dtype=jnp.int32) * cap)[:, None, None] \
            + jnp.arange(cap, dtype=jnp.int32)[None, None, :]
        if gi == 0:
            tabs = _combine_tables(tok.reshape(n_sets, ne * cap), yrow.reshape(n_sets, ne * cap), set_len)
            calls = [(s, tuple(t[s] for t in tabs), row_off + s * set_len) for s in range(n_sets)]
        else:
            tok = tok + (jnp.arange(n_sets, dtype=jnp.int32) * set_len)[:, None, None]
            tabs = _combine_tables(tok.reshape(1, -1), yrow.reshape(1, -1), n_sets * set_len)
            calls = [(gate.shape[0] - 1, tuple(t[0] for t in tabs), row_off)]
        for grp, tab, off in calls:
            if next_mod is not None:
                n_gain, n_shift, n_scale = next_mod
                hall, x_next = _combine(hall, y2d, gate[grp:grp + 1], *tab, off, ne,
                                        modulate=(n_gain, n_shift[grp:grp + 1], n_scale[grp:grp + 1], x_next))
            else:
                hall = _combine(hall, y2d, gate[grp:grp + 1], *tab, off, ne, norm_gain=norm_gain)
    return hall, x_next


def kernel(x, c, ctx, c_ctx, mod_w, mod_b, norm_mix, norm_ffn, ab_w_in, gla_gate_w, gla_gate_b, gla_norm,
           sg_norm_w, sg_norm_b, sg_ws, sg_bs, ab_w_out, c_w_in, sinks, c_w_out, router_w, moe_w_gate,
           moe_w_up, moe_w_down, final_norm):
    B, T, D = x.shape
    L = ctx.shape[1]
    ML, MC = B * T, B * L
    cos, sin = _axial_rope_tables(T)

    cond = jnp.concatenate([jax.nn.silu(c), jax.nn.silu(c_ctx)[None], jnp.zeros((3, D), x.dtype)], axis=0)
    hall = jnp.concatenate([x.reshape(ML, D), ctx.reshape(MC, D)], axis=0)

    all_mods = [_mm_bias(cond.astype(BF), mod_w, layer, mod_b[layer].reshape(1, N_MOD * D), 512)[:B + 1]
                .reshape(B + 1, N_MOD, 1, D) for layer in range(DEPTH)]
    xall = _norm_mod(hall, norm_mix[0], all_mods[0][:, 0], all_mods[0][:, 1], T, 512, BF)

    for layer in range(DEPTH):
        last = layer == DEPTH - 1
        mods = all_mods[layer]
        mod = lambda i: mods[:, i]

        if layer % 2 == 0:
            e = layer // 2
            w_in = ab_w_in[e]
            w_main = jnp.concatenate([w_in[:, :6144], w_in[:, 6176:]], axis=1).astype(BF)
            w_lr = jnp.pad(w_in[:, 6144:6176], ((0, 0), (0, 96))).astype(BF)
            p = _mm(xall, w_main, 1024, 1024)
            lr = _mm(xall, w_lr, 1024, LANES)
            gw = [jnp.zeros((LANES, GLA_K_W), F32).at[d * GLA_RANK:(d + 1) * GLA_RANK].set(gla_gate_w[e, d]).astype(BF)
                  for d in range(2)]
            o_fwd = _gla(p, lr, gw[0], gla_gate_b[e, 0:1], None, None, B, T, L, reverse=False)
            a_mix = _gla(p, lr, gw[1], gla_gate_b[e, 1:2], o_fwd, gla_norm[e].reshape(1, GLA_V_W), B, T, L,
                         reverse=True)
            b_mix = _spatial_gate(p, sg_norm_w[e].reshape(1, SG_W), sg_norm_b[e].reshape(1, SG_W), sg_ws[e],
                                  sg_bs[e].reshape(SG_GROUPS, SG_CHUNK, 1))
            hall = _mm_res2(a_mix, b_mix, ab_w_out[e].astype(BF), hall, mod(2), T, 1024, 1024)
        else:
            o = layer // 2
            p = _mm(xall, c_w_in[o].astype(BF), 1024, 1024)
            att = _attention(p, cos, sin, sinks[o], B, T, L)
            assert last, "an attention layer that is not the last would also have to update the context stream"
            hall = _mm_res(att, c_w_out[o].astype(BF), hall, mods[:B, 2], T, 1024, 1024)

        rw_t = router_w[layer].T.astype(BF)
        moe_w = (moe_w_gate, moe_w_up, moe_w_down, layer)
        if last:
            hall, _ = _moe(hall, [(0, B, T)], norm_ffn[layer], mods[:B, 3], mods[:B, 4], mods[:B, 5], rw_t, *moe_w,
                           norm_gain=final_norm.reshape(1, D))
        else:
            nxt = all_mods[layer + 1]
            hall, xall = _moe(hall, [(0, B, T), (ML, B, L)], norm_ffn[layer], mod(3), mod(4), mod(5), rw_t, *moe_w,
                              next_mod=(norm_mix[layer + 1].reshape(1, D), nxt[:, 0], nxt[:, 1]), x_next=xall)

    return hall[:ML].reshape(B, T, D)
```

```python
import functools

import jax
import jax.numpy as jnp
import numpy as np
from jax import lax
from jax.experimental import pallas as pl
from jax.experimental.pallas import tpu as pltpu

D_MODEL = 4096
DEPTH = 2
GRID_W = 64
EPS = 1e-6
NEG_INF = -1e30
N_MOD = 6
MIX_W = D_MODEL
GLA_HEADS = 4
GLA_V_W = MIX_W // 2
GLA_DV = GLA_V_W // GLA_HEADS
GLA_K_W = GLA_V_W // 2
GLA_DK = GLA_K_W // GLA_HEADS
GLA_RANK = 16
GLA_TAU = 16.0
GLA_CHUNK = 64
SG_GROUPS = 4
SG_W = MIX_W // 2
SG_GC = SG_W // SG_GROUPS
SG_CHUNK = 128
ATT_HEADS = D_MODEL // 128
KV_HEADS = 8
HEAD_DIM = 128
GQ = ATT_HEADS // KV_HEADS
WINDOW = 128
ATT_BLOCK = 128
AXIS_DIM = HEAD_DIM // 2
ROPE_BASE = 10000.0
Q_W = ATT_HEADS * HEAD_DIM
KV_W = KV_HEADS * HEAD_DIM
N_EXPERTS = 16
EC_CAPACITY = 2

V7X_VMEM_LIMIT_BYTES = 56 * 1024 * 1024
LANES = 128

BF = jnp.bfloat16
F32 = jnp.float32


def _cparams(*sem, vmem_limit_bytes=V7X_VMEM_LIMIT_BYTES):
    return pltpu.CompilerParams(dimension_semantics=sem, vmem_limit_bytes=vmem_limit_bytes)


def _mm_body(x_ref, w_ref, o_ref):
    o_ref[...] = jnp.dot(x_ref[...], w_ref[...], preferred_element_type=F32).astype(o_ref.dtype)


def _mm(x, w, tm, tn, out_dtype=F32):
    m, k = x.shape
    n = w.shape[1]
    assert m % tm == 0 and n % tn == 0, (m, n, tm, tn)
    return pl.pallas_call(
        _mm_body,
        grid=(m // tm, n // tn),
        in_specs=[pl.BlockSpec((tm, k), lambda i, j: (i, 0)),
                  pl.BlockSpec((k, tn), lambda i, j: (0, j))],
        out_specs=pl.BlockSpec((tm, tn), lambda i, j: (i, j)),
        out_shape=jax.ShapeDtypeStruct((m, n), out_dtype),
        compiler_params=_cparams("parallel", "arbitrary"),
        name="mm",
    )(x, w)


def _mm_bias_body(x_ref, w_ref, b_ref, o_ref):
    o_ref[...] = jnp.dot(x_ref[...], w_ref[...].astype(BF), preferred_element_type=F32) + b_ref[...]


def _mm_bias(x, w, layer, bias, tn):
    m, k = x.shape
    n = w.shape[2]
    assert n % tn == 0
    return pl.pallas_call(
        _mm_bias_body,
        grid=(n // tn,),
        in_specs=[pl.BlockSpec((m, k), lambda j: (0, 0)),
                  pl.BlockSpec((None, k, tn), lambda j: (layer, 0, j)),
                  pl.BlockSpec((1, tn), lambda j: (0, j))],
        out_specs=pl.BlockSpec((m, tn), lambda j: (0, j)),
        out_shape=jax.ShapeDtypeStruct((m, n), F32),
        compiler_params=_cparams("parallel"),
        name="mm_bias",
    )(x, w, bias)


def _mm_res_body(x_ref, w_ref, r_ref, g_ref, o_ref):
    acc = jnp.dot(x_ref[...], w_ref[...], preferred_element_type=F32)
    o_ref[...] = r_ref[...] + g_ref[0] * acc


def _mm_res(x, w, res, gate, rows_per_group, tm, tn):
    m, k = x.shape
    n = w.shape[1]
    g = gate.shape[0]
    per = rows_per_group // tm
    assert m % tm == 0 and n % tn == 0 and rows_per_group % tm == 0
    return pl.pallas_call(
        _mm_res_body,
        grid=(m // tm, n // tn),
        in_specs=[pl.BlockSpec((tm, k), lambda i, j: (i, 0)),
                  pl.BlockSpec((k, tn), lambda i, j: (0, j)),
                  pl.BlockSpec((tm, tn), lambda i, j: (i, j)),
                  pl.BlockSpec((1, 1, tn), lambda i, j: (jnp.minimum(i // per, g - 1), 0, j))],
        out_specs=pl.BlockSpec((tm, tn), lambda i, j: (i, j)),
        out_shape=jax.ShapeDtypeStruct((m, n), F32),
        compiler_params=_cparams("parallel", "arbitrary"),
        name="mm_res",
    )(x, w, res, gate)


def _modulated(x, gain, shift, scale):
    y = x * lax.rsqrt(jnp.mean(x * x, axis=-1, keepdims=True) + EPS)
    return (y * gain) * (1.0 + scale) + shift


def _norm_mod_body(x_ref, g_ref, sh_ref, sc_ref, o_ref):
    o_ref[...] = _modulated(x_ref[...], g_ref[...], sh_ref[0], sc_ref[0]).astype(o_ref.dtype)


def _group_index(rows_per_group, tm, groups):
    per = rows_per_group // tm
    assert rows_per_group % tm == 0
    return lambda i: (jnp.minimum(i // per, groups - 1), 0, 0)


def _norm_mod(h, gain, shift, scale, rows_per_group, tm, out_dtype):
    m, d = h.shape
    grp = _group_index(rows_per_group, tm, shift.shape[0])
    return pl.pallas_call(
        _norm_mod_body,
        grid=(m // tm,),
        in_specs=[pl.BlockSpec((tm, d), lambda i: (i, 0)),
                  pl.BlockSpec((1, d), lambda i: (0, 0)),
                  pl.BlockSpec((1, 1, d), grp),
                  pl.BlockSpec((1, 1, d), grp)],
        out_specs=pl.BlockSpec((tm, d), lambda i: (i, 0)),
        out_shape=jax.ShapeDtypeStruct((m, d), out_dtype),
        compiler_params=_cparams("parallel"),
        name="norm_mod",
    )(h, gain.reshape(1, d), shift, scale)


def _router_body(x_ref, g_ref, sh_ref, sc_ref, rwt_ref, hm_ref, aff_ref):
    hm = _modulated(x_ref[...], g_ref[...], sh_ref[0], sc_ref[0])
    hm_ref[...] = hm
    logits = lax.dot_general(rwt_ref[...], hm.astype(BF), (((1,), (1,)), ((), ())),
                             preferred_element_type=F32)
    e = jnp.exp(logits - jnp.max(logits, axis=0, keepdims=True))
    aff_ref[...] = e / jnp.sum(e, axis=0, keepdims=True)


def _router(h, gain, shift, scale, rw_t, rows_per_group, tm):
    m, d = h.shape
    ne = rw_t.shape[0]
    grp = _group_index(rows_per_group, tm, shift.shape[0])
    return pl.pallas_call(
        _router_body,
        grid=(m // tm,),
        in_specs=[pl.BlockSpec((tm, d), lambda i: (i, 0)),
                  pl.BlockSpec((1, d), lambda i: (0, 0)),
                  pl.BlockSpec((1, 1, d), grp),
                  pl.BlockSpec((1, 1, d), grp),
                  pl.BlockSpec((ne, d), lambda i: (0, 0))],
        out_specs=[pl.BlockSpec((tm, d), lambda i: (i, 0)),
                   pl.BlockSpec((ne, tm), lambda i: (0, i))],
        out_shape=[jax.ShapeDtypeStruct((m, d), F32), jax.ShapeDtypeStruct((ne, m), F32)],
        compiler_params=_cparams("parallel"),
        name="router",
    )(h, gain.reshape(1, d), shift, scale, rw_t)


GATHER_ROWS = 128
GATHER_BUFS = 3
EXPERT_K_SPLIT = 2
EXPERTS_VMEM_LIMIT_BYTES = 60 * 1024 * 1024


def _experts_body(idx_vm, idx_next_vm, gates_ref, hm_hbm, wg_ref, wu_ref, wd_ref, y_ref,
                  idx_sm, xg, hid, g_acc, u_acc, stage, sem, *, rows, mm_rows, n_ff_tiles, n_d_tiles):
    e = pl.program_id(0)
    j = pl.program_id(1)
    n_experts = pl.num_programs(0)
    ch, nb, ks = GATHER_ROWS, GATHER_BUFS, EXPERT_K_SPLIT
    n_chunks = rows // ch
    tf = hid.shape[2]
    kd = xg.shape[2]
    n_up, n_down = n_ff_tiles * ks, n_d_tiles * ks

    def issue(q):
        def body(r, carry):
            src = idx_sm[0, q * ch + r]
            pltpu.make_async_copy(hm_hbm.at[pl.ds(src, 1)], stage.at[q % nb, pl.ds(r, 1)], sem.at[q % nb]).start()
            return carry
        lax.fori_loop(0, ch, body, 0, unroll=8)

    def finish(q):
        pltpu.make_async_copy(hm_hbm.at[pl.ds(0, ch)], stage.at[q % nb], sem.at[q % nb]).wait()
        for kh in range(ks):
            xg[kh, q * ch:(q + 1) * ch, :] = stage[q % nb, :, kh * kd:(kh + 1) * kd].astype(BF)

    def gather(lo, hi, first):
        if first:
            for q in range(min(nb, n_chunks)):
                issue(q)
        for q in range(lo, hi):
            finish(q)
            if q + nb < n_chunks:
                issue(q + nb)

    @pl.when((e == 0) & (j == 0))
    def _first_gather():
        pltpu.sync_copy(idx_vm.at[0], idx_sm)
        gather(0, n_chunks, True)

    for k in range(n_down):
        lo, hi = k * n_chunks // n_down, (k + 1) * n_chunks // n_down
        if k == 0 or hi > lo:
            @pl.when((j == n_up + k) & (e + 1 < n_experts))
            def _next_gather(k=k, lo=lo, hi=hi):
                if k == 0:
                    pltpu.sync_copy(idx_next_vm.at[0], idx_sm)
                gather(lo, hi, k == 0)

    @pl.when(j < n_up)
    def _up():
        f, kh = j // ks, j % ks
        w = jnp.concatenate([wg_ref[0, 0, 0].astype(BF), wu_ref[0, 0, 0].astype(BF)], axis=1)
        for rc in range(rows // mm_rows):
            rs = slice(rc * mm_rows, (rc + 1) * mm_rows)
            gu = jnp.dot(xg[kh, rs, :], w, preferred_element_type=F32)
            g, u = gu[:, :tf], gu[:, tf:]

            @pl.when(kh == 0)
            def _start():
                g_acc[rs, :] = g
                u_acc[rs, :] = u

            @pl.when((kh > 0) & (kh < ks - 1))
            def _accumulate():
                g_acc[rs, :] += g
                u_acc[rs, :] += u

            @pl.when(kh == ks - 1)
            def _finish():
                gt = g_acc[rs, :] + g
                ut = u_acc[rs, :] + u
                hid[f, rs, :] = ((gt * jax.nn.sigmoid(gt)) * ut).astype(BF)

    @pl.when(j >= n_up)
    def _down():
        kh = (j - n_up) % ks
        per = n_ff_tiles // ks
        wd = wd_ref[0, 0, 0].astype(BF)
        for rc in range(rows // mm_rows):
            rs = slice(rc * mm_rows, (rc + 1) * mm_rows)
            acc = jnp.dot(hid[kh * per, rs, :], wd[0:tf, :], preferred_element_type=F32)
            for f in range(1, per):
                acc = acc + jnp.dot(hid[kh * per + f, rs, :], wd[f * tf:(f + 1) * tf, :], preferred_element_type=F32)
            acc = acc * gates_ref[0, rs, :]

            @pl.when(kh == 0)
            def _start():
                y_ref[0, rs, :] = acc

            @pl.when(kh > 0)
            def _accumulate():
                y_ref[0, rs, :] += acc


def _experts(hm, idx, gates, wg, wu, wd, layer, tf=256, td=512):
    ks = EXPERT_K_SPLIT
    ne, _, rows = idx.shape
    d = hm.shape[1]
    ff = wg.shape[3]
    n_ff_tiles, n_d_tiles = ff // tf, d // td
    n_up, n_down = n_ff_tiles * ks, n_d_tiles * ks
    mm_rows = rows
    assert rows % GATHER_ROWS == 0 and mm_rows % 8 == 0 and n_ff_tiles % ks == 0 and d % ks == 0
    body = functools.partial(_experts_body, rows=rows, mm_rows=mm_rows, n_ff_tiles=n_ff_tiles, n_d_tiles=n_d_tiles)
    up = lambda j: jnp.minimum(j, n_up - 1)
    down = lambda j: jnp.maximum(j - n_up, 0)
    wg4 = wg.reshape(wg.shape[0], ne, ks, d // ks, ff)
    wu4 = wu.reshape(wu.shape[0], ne, ks, d // ks, ff)
    wd4 = wd.reshape(wd.shape[0], ne, ks, ff // ks, d)
    up_spec = pl.BlockSpec((1, 1, 1, d // ks, tf), lambda e, j: (layer, e, up(j) % ks, 0, up(j) // ks))
    return pl.pallas_call(
        body,
        grid=(ne, n_up + n_down),
        in_specs=[pl.BlockSpec((1, 1, rows), lambda e, j: (e, 0, 0)),
                  pl.BlockSpec((1, 1, rows), lambda e, j: (jnp.minimum(e + 1, ne - 1), 0, 0)),
                  pl.BlockSpec((1, rows, 1), lambda e, j: (e, 0, 0)),
                  pl.BlockSpec(memory_space=pl.ANY),
                  up_spec,
                  up_spec,
                  pl.BlockSpec((1, 1, 1, ff // ks, td), lambda e, j: (layer, e, down(j) % ks, 0, down(j) // ks))],
        out_specs=pl.BlockSpec((1, rows, td), lambda e, j: (e, 0, down(j) // ks)),
        out_shape=jax.ShapeDtypeStruct((ne, rows, d), F32),
        scratch_shapes=[pltpu.SMEM((1, rows), jnp.int32),
                        pltpu.VMEM((ks, rows, d // ks), BF),
                        pltpu.VMEM((n_ff_tiles, rows, tf), BF),
                        pltpu.VMEM((rows, tf), F32),
                        pltpu.VMEM((rows, tf), F32),
                        pltpu.VMEM((GATHER_BUFS, GATHER_ROWS, d), F32),
                        pltpu.SemaphoreType.DMA((GATHER_BUFS,))],
        compiler_params=_cparams("arbitrary", "arbitrary", vmem_limit_bytes=EXPERTS_VMEM_LIMIT_BYTES),
        name="experts",
    )(idx, idx, gates, hm, wg4, wu4, wd4)


COMBINE_TOKENS = 64
COMBINE_BUFS = 3
COMBINE_STRIP = 512
COMBINE_UNROLL = 4


def _combine_body(pk_ref, ts_ref, mc_ref, h_ref, cnt_ref, gate_ref, y_hbm, *rest, n_tiles, n_slots, epilogue):
    if epilogue == "modulate":
        ng_ref, nsh_ref, nsc_ref, _, o_ref, x_ref, buf, sem = rest
    elif epilogue == "norm":
        ng_ref, o_ref, buf, sem = rest
    else:
        o_ref, buf, sem = rest
    g = pl.program_id(0)
    tt, d = h_ref.shape
    nb = buf.shape[0]

    def issue(t):
        slot = t % nb

        def start(p):
            v = pk_ref[p]
            src = jnp.bitwise_and(v, 0xFFFF)
            dst = jnp.right_shift(v, 16)
            pltpu.make_async_copy(y_hbm.at[pl.ds(src, 1)], buf.at[slot, pl.ds(dst, 1)], sem.at[slot]).start()

        lo, hi = ts_ref[t], ts_ref[t + 1]
        n_groups = (hi - lo) // COMBINE_UNROLL

        def group(i, carry):
            for u in range(COMBINE_UNROLL):
                start(lo + i * COMBINE_UNROLL + u)
            return carry
        lax.fori_loop(0, n_groups, group, 0)

        def single(p, carry):
            start(p)
            return carry
        lax.fori_loop(lo + n_groups * COMBINE_UNROLL, hi, single, 0)

    def drain(t):
        slot = t % nb
        n = ts_ref[t + 1] - ts_ref[t]
        for bit in range((n_slots * tt).bit_length()):
            @pl.when(jnp.bitwise_and(n, 1 << bit) != 0)
            def _wait(bit=bit):
                rows = pl.ds(0, 1 << bit)
                pltpu.make_async_copy(y_hbm.at[rows], buf.at[slot, rows], sem.at[slot]).wait()

    @pl.when(g == 0)
    def _first():
        def zero(r, carry):
            buf[:, pl.ds(pl.multiple_of(r * tt, tt), tt), :] = jnp.zeros((nb, tt, d), F32)
            return carry
        lax.fori_loop(0, n_slots, zero, 0)
        for t in range(min(nb - 1, n_tiles)):
            issue(t)

    @pl.when(g + nb - 1 < n_tiles)
    def _prefetch():
        issue(g + nb - 1)

    slot = g % nb
    drain(g)
    n_used = mc_ref[g]
    cnt = jnp.broadcast_to(cnt_ref[...], (tt, COMBINE_STRIP))
    for cs in range(d // COMBINE_STRIP):
        cols = slice(cs * COMBINE_STRIP, (cs + 1) * COMBINE_STRIP)

        def body(k, acc):
            v = buf[slot, pl.ds(pl.multiple_of(k * tt, tt), tt), cols]
            return acc + jnp.where(cnt > k, v, 0.0)

        acc = lax.fori_loop(0, n_used, body, jnp.zeros((tt, COMBINE_STRIP), F32))
        o_ref[:, cols] = h_ref[:, cols] + gate_ref[0][:, cols] * acc

    if epilogue == "modulate":
        x_ref[...] = _modulated(o_ref[...], ng_ref[...], nsh_ref[0], nsc_ref[0]).astype(x_ref.dtype)
    elif epilogue == "norm":
        hn = o_ref[...]
        o_ref[...] = (hn * lax.rsqrt(jnp.mean(hn * hn, axis=-1, keepdims=True) + EPS)) * ng_ref[...]


def _combine_tables(tok, yrow, n_tok):
    tt = COMBINE_TOKENS
    s, n_pairs = tok.shape
    key = jnp.sort(jnp.bitwise_or(jnp.left_shift(tok, 16), yrow), axis=-1)
    tok_s = jnp.right_shift(key, 16)
    src = jnp.bitwise_and(key, 0xFFFF)
    cnt = jnp.sum(tok[:, :, None] == jnp.arange(n_tok, dtype=jnp.int32)[None, None, :], axis=1, dtype=jnp.int32)
    start = jnp.concatenate([jnp.zeros((s, 1), jnp.int32), jnp.cumsum(cnt, axis=1, dtype=jnp.int32)], axis=1)
    pos = jnp.broadcast_to(jnp.arange(n_pairs, dtype=jnp.int32), (s, n_pairs))
    is_first = jnp.concatenate([jnp.ones((s, 1), bool), tok_s[:, 1:] != tok_s[:, :-1]], axis=1)
    k = pos - lax.cummax(jnp.where(is_first, pos, 0), axis=1)
    packed = jnp.bitwise_or(src, jnp.left_shift(k * tt + tok_s % tt, 16))
    return packed, start[:, ::tt], jnp.max(cnt.reshape(s, n_tok // tt, tt), axis=2), cnt


def _combine(h, y2d, gate, packed, tile_start, tile_slots, cnt, row_off, n_slots, norm_gain=None, modulate=None):
    tt = COMBINE_TOKENS
    d = h.shape[1]
    n_tok = cnt.shape[0]
    n_tiles = n_tok // tt
    assert n_tok % tt == 0 and row_off % tt == 0 and y2d.shape[0] <= 0x10000 and n_slots * tt <= 0x7FFF
    assert gate.shape[0] == 1 and (norm_gain is None or modulate is None)
    tile0 = row_off // tt
    rows_spec = pl.BlockSpec((tt, d), lambda g, *_: (tile0 + g, 0))
    vec_spec = pl.BlockSpec((1, d), lambda g, *_: (0, 0))
    vec3_spec = pl.BlockSpec((1, 1, d), lambda g, *_: (0, 0, 0))
    in_specs = [rows_spec, pl.BlockSpec((tt, 1), lambda g, *_: (g, 0)), vec3_spec, pl.BlockSpec(memory_space=pl.ANY)]
    args = [packed, tile_start, tile_slots, h, cnt.reshape(n_tok, 1), gate, y2d]
    out_specs, out_shape, aliases, epilogue = rows_spec, jax.ShapeDtypeStruct(h.shape, F32), {3: 0}, None
    if norm_gain is not None:
        epilogue = "norm"
        in_specs += [vec_spec]
        args += [norm_gain]
    elif modulate is not None:
        epilogue = "modulate"
        gain, shift, scale, x = modulate
        in_specs += [vec_spec, vec3_spec, vec3_spec, pl.BlockSpec(memory_space=pl.ANY)]
        args += [gain, shift, scale, x]
        out_specs = [rows_spec, rows_spec]
        out_shape = [out_shape, jax.ShapeDtypeStruct(x.shape, x.dtype)]
        aliases = {3: 0, len(args) - 1: 1}
    grid_spec = pltpu.PrefetchScalarGridSpec(
        num_scalar_prefetch=3,
        grid=(n_tiles,),
        in_specs=in_specs,
        out_specs=out_specs,
        scratch_shapes=[pltpu.VMEM((COMBINE_BUFS, n_slots * tt, d), F32),
                        pltpu.SemaphoreType.DMA((COMBINE_BUFS,))],
    )
    return pl.pallas_call(
        functools.partial(_combine_body, n_tiles=n_tiles, n_slots=n_slots, epilogue=epilogue),
        grid_spec=grid_spec,
        out_shape=out_shape,
        input_output_aliases=aliases,
        compiler_params=_cparams("arbitrary"),
        name="combine",
    )(*args)


ATT_Q_TILE = 512


def _rope(x, cos, sin):
    lane = lax.broadcasted_iota(jnp.int32, x.shape, 1)
    fwd = pltpu.roll(x, AXIS_DIM // 2, axis=1)
    bwd = pltpu.roll(x, HEAD_DIM - AXIS_DIM // 2, axis=1)
    rot = jnp.where(lane % AXIS_DIM < AXIS_DIM // 2, -bwd, fwd)
    return x * cos + rot * sin


def _attn_body(q_ref, kp_ref, km_ref, kn_ref, vp_ref, vm_ref, vn_ref, kc_ref, vc_ref,
               cm_ref, sm_ref, cp_ref, sp_ref, cn_ref, sn_ref, sink_ref, o_ref, *, n_q_tiles):
    i = pl.program_id(2)
    qt = q_ref.shape[0]
    blk = ATT_BLOCK
    nb = qt // blk
    scale = HEAD_DIM ** -0.5
    cos_w = jnp.concatenate([cp_ref[...], cm_ref[...], cn_ref[...]], axis=0)
    sin_w = jnp.concatenate([sp_ref[...], sm_ref[...], sn_ref[...]], axis=0)
    k_w = jnp.concatenate([kp_ref[...], km_ref[...], kn_ref[...]], axis=0)
    k_w = _rope(k_w, cos_w, sin_w).astype(BF)
    v_w = jnp.concatenate([vp_ref[...], vm_ref[...], vn_ref[...]], axis=0).astype(BF)
    kc = kc_ref[...].astype(BF)
    vc = vc_ref[...].astype(BF)
    sink = sink_ref[0]
    nt = (((1,), (1,)), ((), ()))
    band = 3 * blk
    row = lax.broadcasted_iota(jnp.int32, (GQ * blk, band), 0) % blk
    col = lax.broadcasted_iota(jnp.int32, (GQ * blk, band), 1)
    in_window = (col >= row) & (col <= row + 2 * WINDOW)
    for jb in range(nb):
        rs = slice(jb * blk, (jb + 1) * blk)
        cos_q = cm_ref[rs, :]
        sin_q = sm_ref[rs, :]
        qs = jnp.concatenate([_rope(q_ref[rs, g * HEAD_DIM:(g + 1) * HEAD_DIM], cos_q, sin_q)
                              for g in range(GQ)], axis=0).astype(BF)
        kb = k_w[jb * blk:jb * blk + band]
        vb = v_w[jb * blk:jb * blk + band]
        s_loc = lax.dot_general(qs, kb, nt, preferred_element_type=F32) * scale
        s_ctx = lax.dot_general(qs, kc, nt, preferred_element_type=F32) * scale
        valid = in_window
        if jb == 0:
            valid = valid & (col >= jnp.where(i == 0, blk, 0))
        if jb == nb - 1:
            valid = valid & (col < jnp.where(i == n_q_tiles - 1, 2 * blk, band))
        s_loc = jnp.where(valid, s_loc, NEG_INF)
        m = jnp.maximum(jnp.maximum(jnp.max(s_loc, axis=1, keepdims=True),
                                    jnp.max(s_ctx, axis=1, keepdims=True)), sink)
        e_loc = jnp.exp(s_loc - m)
        e_ctx = jnp.exp(s_ctx - m)
        den = (jnp.sum(e_loc, axis=1, keepdims=True) + jnp.sum(e_ctx, axis=1, keepdims=True)
               + jnp.exp(sink - m))
        o = (jnp.dot(e_loc.astype(BF), vb, preferred_element_type=F32)
             + jnp.dot(e_ctx.astype(BF), vc, preferred_element_type=F32)) / den
        for g in range(GQ):
            o_ref[rs, g * HEAD_DIM:(g + 1) * HEAD_DIM] = o[g * blk:(g + 1) * blk].astype(o_ref.dtype)


def _attention(p, cos, sin, sinks, n_batch, seq, ctx_len):
    qt, blk = ATT_Q_TILE, ATT_BLOCK
    ml = n_batch * seq
    nq = seq // qt
    nbk = seq // blk
    bpt = qt // blk
    kcol, vcol = Q_W // HEAD_DIM, (Q_W + KV_W) // HEAD_DIM
    assert seq % qt == 0 and ml % ctx_len == 0
    sink_col = jnp.repeat(sinks.astype(F32).reshape(KV_HEADS, GQ, 1), blk, axis=2).reshape(KV_HEADS, GQ * blk, 1)
    prev_blk = lambda i: jnp.maximum(i * bpt - 1, 0)
    next_blk = lambda i: jnp.minimum((i + 1) * bpt, nbk - 1)
    qspec = pl.BlockSpec((qt, GQ * HEAD_DIM), lambda b, h, i: (b * nq + i, h))

    def kv_specs(c0):
        return [pl.BlockSpec((blk, HEAD_DIM), lambda b, h, i: (b * nbk + prev_blk(i), c0 + h)),
                pl.BlockSpec((qt, HEAD_DIM), lambda b, h, i: (b * nq + i, c0 + h)),
                pl.BlockSpec((blk, HEAD_DIM), lambda b, h, i: (b * nbk + next_blk(i), c0 + h))]

    ctx_spec = lambda c0: pl.BlockSpec((ctx_len, HEAD_DIM), lambda b, h, i: (ml // ctx_len + b, c0 + h))
    tab_specs = [pl.BlockSpec((qt, HEAD_DIM), lambda b, h, i: (i, 0)),
                 pl.BlockSpec((qt, HEAD_DIM), lambda b, h, i: (i, 0)),
                 pl.BlockSpec((blk, HEAD_DIM), lambda b, h, i: (prev_blk(i), 0)),
                 pl.BlockSpec((blk, HEAD_DIM), lambda b, h, i: (prev_blk(i), 0)),
                 pl.BlockSpec((blk, HEAD_DIM), lambda b, h, i: (next_blk(i), 0)),
                 pl.BlockSpec((blk, HEAD_DIM), lambda b, h, i: (next_blk(i), 0))]
    return pl.pallas_call(
        functools.partial(_attn_body, n_q_tiles=nq),
        grid=(n_batch, KV_HEADS, nq),
        in_specs=[qspec] + kv_specs(kcol) + kv_specs(vcol) + [ctx_spec(kcol), ctx_spec(vcol)] + tab_specs
                 + [pl.BlockSpec((1, GQ * blk, 1), lambda b, h, i: (h, 0, 0))],
        out_specs=pl.BlockSpec((qt, GQ * HEAD_DIM), lambda b, h, i: (b * nq + i, h)),
        out_shape=jax.ShapeDtypeStruct((ml, Q_W), BF),
        compiler_params=_cparams("parallel", "parallel", "arbitrary"),
        name="attention",
    )(p, p, p, p, p, p, p, p, p, cos, sin, cos, sin, cos, sin, sink_col)


GLA_TOKENS = 256
_NT = (((1,), (1,)), ((), ()))
_TN = (((0,), (0,)), ((), ()))


def _gla_body(*refs, reverse, fuse_output):
    if fuse_output:
        q_ref, k_ref, v_ref, lr_ref, gw_ref, gb_ref, of_ref, g_ref, gain_ref, o_ref, st_ref = refs
    else:
        q_ref, k_ref, v_ref, lr_ref, gw_ref, gb_ref, o_ref, st_ref = refs
    tb = q_ref.shape[0]
    ch = GLA_CHUNK
    n_ch = tb // ch

    @pl.when(pl.program_id(1) == 0)
    def _reset():
        st_ref[...] = jnp.zeros(st_ref.shape, F32)

    r = lax.broadcasted_iota(jnp.int32, (tb, tb), 0)
    c = lax.broadcasted_iota(jnp.int32, (tb, tb), 1)
    same = (r // ch) == (c // ch)
    tri = same & ((c >= r) if reverse else (c <= r))
    tri_bf = jnp.where(tri, 1.0, 0.0).astype(BF)
    same_bf = jnp.where(same, 1.0, 0.0).astype(BF)

    z = jnp.dot(lr_ref[...].astype(BF), gw_ref[...], preferred_element_type=F32) + gb_ref[...]
    la = jax.nn.log_sigmoid(z) * (1.0 / GLA_TAU)
    la_hi = la.astype(BF)
    la_lo = (la - la_hi.astype(F32)).astype(BF)
    b = (jnp.dot(tri_bf, la_hi, preferred_element_type=F32)
         + jnp.dot(tri_bf, la_lo, preferred_element_type=F32))
    tot = (jnp.dot(same_bf, la_hi, preferred_element_type=F32)
           + jnp.dot(same_bf, la_lo, preferred_element_type=F32))
    q = q_ref[...] * (GLA_DK ** -0.5)
    k = k_ref[...]
    v = v_ref[...].astype(BF)
    q_in = (q * jnp.exp(b)).astype(BF)
    k_in = (k * jnp.exp(-b)).astype(BF)
    k_dec = (k * jnp.exp(tot - b)).astype(BF)
    dec = jnp.exp(tot)
    for h in range(GLA_HEADS):
        ks = slice(h * GLA_DK, (h + 1) * GLA_DK)
        vs = slice(h * GLA_DV, (h + 1) * GLA_DV)
        sc = lax.dot_general(q_in[:, ks], k_in[:, ks], _NT, preferred_element_type=F32)
        sc = jnp.where(tri, sc, 0.0).astype(BF)
        o_intra = jnp.dot(sc, v[:, vs], preferred_element_type=F32)
        outs = [None] * n_ch
        for ci in (range(n_ch - 1, -1, -1) if reverse else range(n_ch)):
            rs = slice(ci * ch, (ci + 1) * ch)
            st = st_ref[h]
            outs[ci] = o_intra[rs] + lax.dot_general(q_in[rs, ks], st.astype(BF), _NT, preferred_element_type=F32)
            upd = lax.dot_general(v[rs, vs], k_dec[rs, ks], _TN, preferred_element_type=F32)
            st_ref[h] = st * dec[ci * ch:ci * ch + 1, ks] + upd
        o = jnp.concatenate(outs, axis=0)
        if fuse_output:
            o = o + of_ref[:, vs]
            y = o * lax.rsqrt(jnp.mean(o * o, axis=-1, keepdims=True) + EPS) * gain_ref[:, vs]
            g = g_ref[:, vs]
            o_ref[:, vs] = (y * (g * jax.nn.sigmoid(g))).astype(o_ref.dtype)
        else:
            o_ref[:, vs] = o


def _gla(p, lr, gate_w, gate_b, o_fwd, gain, n_batch, seq, ctx_len, *, reverse):
    tb = GLA_TOKENS
    ml = n_batch * seq
    m = p.shape[0]
    nc, nl = ctx_len // tb, seq // tb
    assert ctx_len % tb == 0 and seq % tb == 0 and ml % tb == 0
    ctx_base = ml // tb

    def row(b, s):
        sc_ = (nc - 1 - s) if reverse else s
        sl_ = (nl - 1 - (s - nc)) if reverse else (s - nc)
        return jnp.where(s < nc, ctx_base + b * nc + sc_, b * nl + sl_)

    kk, kv, kg = 1, 2 * GLA_K_W // GLA_V_W, (2 * GLA_K_W + GLA_V_W) // GLA_V_W
    in_specs = [pl.BlockSpec((tb, GLA_K_W), lambda b, s: (row(b, s), 0)),
                pl.BlockSpec((tb, GLA_K_W), lambda b, s: (row(b, s), kk)),
                pl.BlockSpec((tb, GLA_V_W), lambda b, s: (row(b, s), kv)),
                pl.BlockSpec((tb, LANES), lambda b, s: (row(b, s), 0)),
                pl.BlockSpec((LANES, GLA_K_W), lambda b, s: (0, 0)),
                pl.BlockSpec((1, GLA_K_W), lambda b, s: (0, 0))]
    args = [p, p, p, lr, gate_w, gate_b]
    if reverse:
        in_specs += [pl.BlockSpec((tb, GLA_V_W), lambda b, s: (row(b, s), 0)),
                     pl.BlockSpec((tb, GLA_V_W), lambda b, s: (row(b, s), kg)),
                     pl.BlockSpec((1, GLA_V_W), lambda b, s: (0, 0))]
        args += [o_fwd, p, gain]
    return pl.pallas_call(
        functools.partial(_gla_body, reverse=reverse, fuse_output=reverse),
        grid=(n_batch, nc + nl),
        in_specs=in_specs,
        out_specs=pl.BlockSpec((tb, GLA_V_W), lambda b, s: (row(b, s), 0)),
        out_shape=jax.ShapeDtypeStruct((m, GLA_V_W), BF if reverse else F32),
        scratch_shapes=[pltpu.VMEM((GLA_HEADS, GLA_DV, GLA_DK), F32)],
        compiler_params=_cparams("parallel", "arbitrary"),
        name="gla_bwd" if reverse else "gla_fwd",
    )(*args)


SG_TOKENS = 512


def _sg_body(u_ref, vs_ref, nw_ref, nb_ref, ws_ref, bs_ref, o_ref):
    u = jax.nn.gelu(u_ref[...])
    vf = jax.nn.gelu(vs_ref[...])
    mu = jnp.mean(vf, axis=-1, keepdims=True)
    var = jnp.mean(jnp.square(vf - mu), axis=-1, keepdims=True)
    vn = ((vf - mu) * lax.rsqrt(var + EPS) * nw_ref[...] + nb_ref[...]).astype(BF)
    ws = ws_ref[0].astype(BF)
    bias = bs_ref[0]
    for ci in range(u.shape[0] // SG_CHUNK):
        rs = slice(ci * SG_CHUNK, (ci + 1) * SG_CHUNK)
        s = jnp.dot(ws, vn[rs], preferred_element_type=F32) + bias
        o_ref[rs, :] = (u[rs] * s).astype(o_ref.dtype)


def _spatial_gate(p, nw, nb, ws, bias_col):
    m = p.shape[0]
    tt = SG_TOKENS
    ku = (2 * GLA_K_W + 2 * GLA_V_W) // SG_GC
    kv = ku + SG_W // SG_GC
    return pl.pallas_call(
        _sg_body,
        grid=(m // tt, SG_GROUPS),
        in_specs=[pl.BlockSpec((tt, SG_GC), lambda i, g: (i, ku + g)),
                  pl.BlockSpec((tt, SG_GC), lambda i, g: (i, kv + g)),
                  pl.BlockSpec((1, SG_GC), lambda i, g: (0, g)),
                  pl.BlockSpec((1, SG_GC), lambda i, g: (0, g)),
                  pl.BlockSpec((1, SG_CHUNK, SG_CHUNK), lambda i, g: (g, 0, 0)),
                  pl.BlockSpec((1, SG_CHUNK, 1), lambda i, g: (g, 0, 0))],
        out_specs=pl.BlockSpec((tt, SG_GC), lambda i, g: (i, g)),
        out_shape=jax.ShapeDtypeStruct((m, SG_W), BF),
        compiler_params=_cparams("parallel", "arbitrary"),
        name="spatial_gate",
    )(p, p, nw, nb, ws, bias_col)


def _mm_res2_body(a_ref, b_ref, wa_ref, wb_ref, r_ref, g_ref, o_ref):
    acc = (jnp.dot(a_ref[...], wa_ref[...], preferred_element_type=F32)
           + jnp.dot(b_ref[...], wb_ref[...], preferred_element_type=F32))
    o_ref[...] = r_ref[...] + g_ref[0] * acc


def _mm_res2(a, b, w, res, gate, rows_per_group, tm, tn):
    m, ka = a.shape
    kb = b.shape[1]
    n = w.shape[1]
    g = gate.shape[0]
    per = rows_per_group // tm
    assert ka == kb and w.shape[0] == ka + kb and m % tm == 0 and n % tn == 0 and rows_per_group % tm == 0
    return pl.pallas_call(
        _mm_res2_body,
        grid=(m // tm, n // tn),
        in_specs=[pl.BlockSpec((tm, ka), lambda i, j: (i, 0)),
                  pl.BlockSpec((tm, kb), lambda i, j: (i, 0)),
                  pl.BlockSpec((ka, tn), lambda i, j: (0, j)),
                  pl.BlockSpec((kb, tn), lambda i, j: (1, j)),
                  pl.BlockSpec((tm, tn), lambda i, j: (i, j)),
                  pl.BlockSpec((1, 1, tn), lambda i, j: (jnp.minimum(i // per, g - 1), 0, j))],
        out_specs=pl.BlockSpec((tm, tn), lambda i, j: (i, j)),
        out_shape=jax.ShapeDtypeStruct((m, n), F32),
        compiler_params=_cparams("parallel", "arbitrary"),
        name="mm_res2",
    )(a, b, w, w, res, gate)


def _axial_rope_tables(T):
    rows = T // GRID_W
    row, col = jnp.meshgrid(jnp.arange(rows), jnp.arange(GRID_W), indexing='ij')
    inv_freq = ROPE_BASE ** (-jnp.arange(0, AXIS_DIM, 2, dtype=jnp.float32) / AXIS_DIM)

    def angles(pos):
        ang = pos.reshape(-1).astype(jnp.float32)[:, None] * inv_freq
        return jnp.concatenate([ang, ang], axis=-1)

    ang = jnp.concatenate([angles(row), angles(col)], axis=-1)
    return jnp.cos(ang), jnp.sin(ang)


def _moe(hall, groups, gain, shift, scale, gate, rw_t, wg, wu, wd, layer, norm_gain=None, next_mod=None, x_next=None):
    m, d = hall.shape
    ne = rw_t.shape[0]
    set_len0 = groups[0][2]
    hm, aff = _router(hall, gain, shift, scale, rw_t, set_len0, 512)
    idx_parts, gate_parts, meta = [], [], []
    r_off = 0
    for row_off, n_sets, set_len in groups:
        cap = EC_CAPACITY * set_len // ne
        a = aff[:, row_off:row_off + n_sets * set_len].reshape(ne, n_sets, set_len)
        gts, idx = lax.top_k(a, cap)
        rows = idx + (row_off + jnp.arange(n_sets, dtype=jnp.int32) * set_len)[None, :, None]
        idx_parts.append(rows.reshape(ne, n_sets * cap))
        gate_parts.append(gts.reshape(ne, n_sets * cap))
        meta.append((row_off, n_sets, set_len, cap, r_off, idx))
        r_off += n_sets * cap
    rows_per_expert = r_off
    idx_all = jnp.concatenate(idx_parts, axis=1).astype(jnp.int32)
    gates_all = jnp.concatenate(gate_parts, axis=1)
    y = _experts(hm, idx_all.reshape(ne, 1, rows_per_expert), gates_all.reshape(ne, rows_per_expert, 1),
                 wg, wu, wd, layer)
    y2d = y.reshape(ne * rows_per_expert, d)
    e_base = (jnp.arange(ne, dtype=jnp.int32) * rows_per_expert)[None, :, None]
    for gi, (row_off, n_sets, set_len, cap, r0, idx) in enumerate(meta):
        tok = jnp.swapaxes(idx, 0, 1).astype(jnp.int32)
        yrow = e_base + r0 + (jnp.arange(n_sets, dtype=jnp.int32) * cap)[:, None, None] \
            + jnp.arange(cap, dtype=jnp.int32)[None, None, :]
        if gi == 0:
            tabs = _combine_tables(tok.reshape(n_sets, ne * cap), yrow.reshape(n_sets, ne * cap), set_len)
            calls = [(s, tuple(t[s] for t in tabs), row_off + s * set_len) for s in range(n_sets)]
        else:
            tok = tok + (jnp.arange(n_sets, dtype=jnp.int32) * set_len)[:, None, None]
            tabs = _combine_tables(tok.reshape(1, -1), yrow.reshape(1, -1), n_sets * set_len)
            calls = [(gate.shape[0] - 1, tuple(t[0] for t in tabs), row_off)]
        for grp, tab, off in calls:
            if next_mod is not None:
                n_gain, n_shift, n_scale = next_mod
                hall, x_next = _combine(hall, y2d, gate[grp:grp + 1], *tab, off, ne,
                                        modulate=(n_gain, n_shift[grp:grp + 1], n_scale[grp:grp + 1], x_next))
            else:
                hall = _combine(hall, y2d, gate[grp:grp + 1], *tab, off, ne, norm_gain=norm_gain)
    return hall, x_next


def kernel(x, c, ctx, c_ctx, mod_w, mod_b, norm_mix, norm_ffn, ab_w_in, gla_gate_w, gla_gate_b, gla_norm,
           sg_norm_w, sg_norm_b, sg_ws, sg_bs, ab_w_out, c_w_in, sinks, c_w_out, router_w, moe_w_gate,
           moe_w_up, moe_w_down, final_norm):
    B, T, D = x.shape
    L = ctx.shape[1]
    ML, MC = B * T, B * L
    cos, sin = _axial_rope_tables(T)

    cond = jnp.concatenate([jax.nn.silu(c), jax.nn.silu(c_ctx)[None], jnp.zeros((3, D), x.dtype)], axis=0)
    hall = jnp.concatenate([x.reshape(ML, D), ctx.reshape(MC, D)], axis=0)

    all_mods = [_mm_bias(cond.astype(BF), mod_w, layer, mod_b[layer].reshape(1, N_MOD * D), 512)[:B + 1]
                .reshape(B + 1, N_MOD, 1, D) for layer in range(DEPTH)]
    xall = _norm_mod(hall, norm_mix[0], all_mods[0][:, 0], all_mods[0][:, 1], T, 512, BF)

    for layer in range(DEPTH):
        last = layer == DEPTH - 1
        mods = all_mods[layer]
        mod = lambda i: mods[:, i]

        if layer % 2 == 0:
            e = layer // 2
            w_in = ab_w_in[e]
            w_main = jnp.concatenate([w_in[:, :6144], w_in[:, 6176:]], axis=1).astype(BF)
            w_lr = jnp.pad(w_in[:, 6144:6176], ((0, 0), (0, 96))).astype(BF)
            p = _mm(xall, w_main, 1024, 1024)
            lr = _mm(xall, w_lr, 1024, LANES)
            gw = [jnp.zeros((LANES, GLA_K_W), F32).at[d * GLA_RANK:(d + 1) * GLA_RANK].set(gla_gate_w[e, d]).astype(BF)
                  for d in range(2)]
            o_fwd = _gla(p, lr, gw[0], gla_gate_b[e, 0:1], None, None, B, T, L, reverse=False)
            a_mix = _gla(p, lr, gw[1], gla_gate_b[e, 1:2], o_fwd, gla_norm[e].reshape(1, GLA_V_W), B, T, L,
                         reverse=True)
            b_mix = _spatial_gate(p, sg_norm_w[e].reshape(1, SG_W), sg_norm_b[e].reshape(1, SG_W), sg_ws[e],
                                  sg_bs[e].reshape(SG_GROUPS, SG_CHUNK, 1))
            hall = _mm_res2(a_mix, b_mix, ab_w_out[e].astype(BF), hall, mod(2), T, 1024, 1024)
        else:
            o = layer // 2
            p = _mm(xall, c_w_in[o].astype(BF), 1024, 1024)
            att = _attention(p, cos, sin, sinks[o], B, T, L)
            assert last, "an attention layer that is not the last would also have to update the context stream"
            hall = _mm_res(att, c_w_out[o].astype(BF), hall, mods[:B, 2], T, 1024, 1024)

        rw_t = router_w[layer].T.astype(BF)
        moe_w = (moe_w_gate, moe_w_up, moe_w_down, layer)
        if last:
            hall, _ = _moe(hall, [(0, B, T)], norm_ffn[layer], mods[:B, 3], mods[:B, 4], mods[:B, 5], rw_t, *moe_w,
                           norm_gain=final_norm.reshape(1, D))
        else:
            nxt = all_mods[layer + 1]
            hall, xall = _moe(hall, [(0, B, T), (ML, B, L)], norm_ffn[layer], mod(3), mod(4), mod(5), rw_t, *moe_w,
                              next_mod=(norm_mix[layer + 1].reshape(1, D), nxt[:, 0], nxt[:, 1]), x_next=xall)

    return hall[:ML].reshape(B, T, D)
```
